```python
import math
import jax, jax.numpy as jnp
from jax import lax
import numpy as np

D_MODEL = 1024
BATCH = 16
SEQ = 2048
DEPTH = 4

MEM_LEN = 256
N_EVEN = (DEPTH + 1) // 2
N_ODD = DEPTH // 2
MIX_WIDTH = D_MODEL
EPS = 1e-6
Q_BLOCK = 128

HGRN_DIM = 128
HGRN_WIDTH = MIX_WIDTH // 2
HGRN_HEADS = HGRN_WIDTH // HGRN_DIM
HGRN_CHUNK = 64

DIFF_WIDTH = MIX_WIDTH - HGRN_WIDTH
DIFF_HEADS = 4
DIFF_V_DIM = DIFF_WIDTH // DIFF_HEADS
DIFF_QK_DIM = DIFF_V_DIM // 2

FOX_WIDTH = MIX_WIDTH // 2
FOX_HEADS = 8
FOX_DIM = FOX_WIDTH // FOX_HEADS

CONV_CH = MIX_WIDTH - FOX_WIDTH
CONV_WIDTH = 31

XATTN_HEADS = 4
XATTN_DIM = D_MODEL // XATTN_HEADS

D_FF = 2816

EVEN_IN = 4 * HGRN_WIDTH + 3 * DIFF_WIDTH
EVEN_SPLITS = [HGRN_WIDTH, 2 * HGRN_WIDTH, 3 * HGRN_WIDTH, 4 * HGRN_WIDTH,
               4 * HGRN_WIDTH + DIFF_WIDTH, 4 * HGRN_WIDTH + 2 * DIFF_WIDTH]
ODD_IN = 3 * FOX_WIDTH + FOX_HEADS + 2 * CONV_CH
ODD_SPLITS = [FOX_WIDTH, 2 * FOX_WIDTH, 3 * FOX_WIDTH, 3 * FOX_WIDTH + FOX_HEADS,
              3 * FOX_WIDTH + FOX_HEADS + CONV_CH]

kernel_name = "hybrid_hgrn2_diffattn_fox_conformer_trunk"


def rms_norm(x, g):
    xf = x.astype(jnp.float32)
    y = xf * lax.rsqrt(jnp.mean(xf * xf, axis=-1, keepdims=True) + EPS)
    return (y * g.astype(jnp.float32)).astype(x.dtype)


def layer_norm(x, g, b):
    xf = x.astype(jnp.float32)
    mu = jnp.mean(xf, axis=-1, keepdims=True)
    var = jnp.mean(jnp.square(xf - mu), axis=-1, keepdims=True)
    y = (xf - mu) * lax.rsqrt(var + EPS) * g.astype(jnp.float32) + b.astype(jnp.float32)
    return y.astype(x.dtype)


def swiglu(x, w_gate, w_up, w_down):
    return (jax.nn.silu(x @ w_gate) * (x @ w_up)) @ w_down


def to_heads(t, n_heads):
    b, s, _ = t.shape
    return t.reshape(b, s, n_heads, -1).transpose(0, 2, 1, 3)


def from_heads(t):
    b, n, s, d = t.shape
    return t.transpose(0, 2, 1, 3).reshape(b, s, n * d)


def alibi_slopes(n):
    return jnp.asarray(2.0 ** (-8.0 * np.arange(1, n + 1) / n), jnp.float32)


def causal_block_attention(q, k, v, bias_fn):
    seq = q.shape[2]
    scale = q.shape[-1] ** -0.5
    outs = []
    for blk in range(seq // Q_BLOCK):
        q0, q1 = blk * Q_BLOCK, (blk + 1) * Q_BLOCK
        logits = jnp.einsum('bgtd,bgsd->bgts', q[:, :, q0:q1], k[:, :, :q1]).astype(jnp.float32) * scale
        logits = logits + bias_fn(q0, q1)
        t_pos = jnp.arange(q0, q1)[:, None]
        s_pos = jnp.arange(q1)[None, :]
        logits = jnp.where(s_pos <= t_pos, logits, -jnp.inf)
        p = jax.nn.softmax(logits, axis=-1).astype(v.dtype)
        outs.append(jnp.einsum('bgts,bgsv->bgtv', p, v[:, :, :q1]))
    return jnp.concatenate(outs, axis=2)


def hgrn2_mixer(q, f_logit, inp, gate, lb, norm_g):
    f32 = jnp.float32
    bsz, seq, _ = q.shape
    lbh = jnp.maximum(lb.astype(f32), 0.0).reshape(HGRN_HEADS, 1, HGRN_DIM)
    z = to_heads(f_logit.astype(f32), HGRN_HEADS)
    logf = jnp.logaddexp(jnp.log(lbh), jnp.log1p(-lbh) + jax.nn.log_sigmoid(z))
    k = -jnp.expm1(logf)
    qh = to_heads(q.astype(f32), HGRN_HEADS)
    vh = to_heads(inp.astype(f32), HGRN_HEADS)
    nc = seq // HGRN_CHUNK

    def chunks(t):
        return t.reshape(bsz, HGRN_HEADS, nc, HGRN_CHUNK, -1).transpose(2, 0, 1, 3, 4)

    causal = jnp.tril(jnp.ones((HGRN_CHUNK, HGRN_CHUNK), bool))[:, :, None]

    def step(state, xs):
        qc, kc, vc, lfc = xs
        b = jnp.cumsum(lfc, axis=-2)
        decay = jnp.exp(jnp.where(causal, b[..., :, None, :] - b[..., None, :, :], -jnp.inf))
        scores = jnp.einsum('bhtd,bhsd,bhtsd->bhts', qc, kc, decay)
        o = (jnp.einsum('bhts,bhsv->bhtv', scores, vc)
             + jnp.einsum('bhtd,bhdv->bhtv', qc * jnp.exp(b), state))
        b_end = b[..., -1:, :]
        state = (jnp.exp(b_end)[..., 0, :, None] * state
                 + jnp.einsum('bhsd,bhsv->bhdv', kc * jnp.exp(b_end - b), vc))
        return state, o

    s0 = jnp.zeros((bsz, HGRN_HEADS, HGRN_DIM, HGRN_DIM), f32)
    _, o = lax.scan(step, s0, (chunks(qh), chunks(k), chunks(vh), chunks(logf)))
    o = o.transpose(1, 0, 3, 2, 4).reshape(bsz, seq, HGRN_HEADS, HGRN_DIM)
    o = rms_norm(o, norm_g.reshape(HGRN_HEADS, HGRN_DIM))
    o = o.reshape(bsz, seq, HGRN_WIDTH) * jax.nn.silu(gate.astype(f32))
    return o.astype(q.dtype)


def diff_attention(q, k, v, lam_p, norm_g, layer_idx):
    f32 = jnp.float32
    bsz, seq, _ = q.shape

    def split_maps(t):
        return t.reshape(bsz, seq, 2 * DIFF_HEADS, DIFF_QK_DIM).transpose(0, 2, 1, 3)

    vh = jnp.repeat(to_heads(v, DIFF_HEADS), 2, axis=1)
    slopes = jnp.repeat(alibi_slopes(DIFF_HEADS), 2)

    def alibi(q0, q1):
        t_pos = jnp.arange(q0, q1, dtype=f32)[:, None]
        s_pos = jnp.arange(q1, dtype=f32)[None, :]
        return -slopes[None, :, None, None] * (t_pos - s_pos)

    o = causal_block_attention(split_maps(q), split_maps(k), vh, alibi)
    o = o.reshape(bsz, DIFF_HEADS, 2, seq, DIFF_V_DIM).astype(f32)
    lam_init = 0.8 - 0.6 * math.exp(-0.3 * layer_idx)
    lp = lam_p.astype(f32)
    lam = jnp.exp(jnp.sum(lp[0] * lp[1])) - jnp.exp(jnp.sum(lp[2] * lp[3])) + lam_init
    o = o[:, :, 0] - lam * o[:, :, 1]
    o = rms_norm(o.transpose(0, 2, 1, 3), norm_g.reshape(DIFF_HEADS, DIFF_V_DIM)) * (1.0 - lam_init)
    return o.reshape(bsz, seq, DIFF_WIDTH).astype(q.dtype)


def even_mixer(hn, w_in, w_out, lb, hgrn_g, lam_p, diff_g, layer_idx):
    proj = hn @ w_in
    a_q, a_f, a_i, a_g, b_q, b_k, b_v = jnp.split(proj, EVEN_SPLITS, axis=-1)
    ya = hgrn2_mixer(a_q, a_f, a_i, a_g, lb, hgrn_g)
    yb = diff_attention(b_q, b_k, b_v, lam_p, diff_g, layer_idx)
    return jnp.concatenate([ya, yb], axis=-1) @ w_out


def odd_mixer(hn, w_in, w_out, f_bias, conv_w, conv_b, ln_g, ln_b):
    f32 = jnp.float32
    proj = hn @ w_in
    c_q, c_k, c_v, c_f, d_a, d_g = jnp.split(proj, ODD_SPLITS, axis=-1)
    logf = jax.nn.log_sigmoid(c_f.astype(f32) + f_bias.astype(f32))
    cum_f = jnp.cumsum(logf, axis=1).transpose(0, 2, 1)

    def forget_bias(q0, q1):
        return cum_f[:, :, q0:q1, None] - cum_f[:, :, None, :q1]

    yc = from_heads(causal_block_attention(to_heads(c_q, FOX_HEADS), to_heads(c_k, FOX_HEADS),
                                           to_heads(c_v, FOX_HEADS), forget_bias))
    u = d_a * jax.nn.sigmoid(d_g)
    conv = lax.conv_general_dilated(u, conv_w[:, None, :].astype(u.dtype), window_strides=(1,),
                                    padding=[(CONV_WIDTH - 1, 0)],
                                    dimension_numbers=('NWC', 'WIO', 'NWC'),
                                    feature_group_count=CONV_CH) + conv_b.astype(u.dtype)
    yd = jax.nn.silu(layer_norm(conv, ln_g, ln_b))
    return jnp.concatenate([yc.astype(hn.dtype), yd.astype(hn.dtype)], axis=-1) @ w_out


def cross_attention(hn, mem, mem_g, w_q, w_kv, w_o):
    q = to_heads(hn @ w_q, XATTN_HEADS)
    k, v = jnp.split(rms_norm(mem, mem_g) @ w_kv, 2, axis=-1)
    kh, vh = to_heads(k, XATTN_HEADS), to_heads(v, XATTN_HEADS)
    logits = jnp.einsum('bhtd,bhsd->bhts', q, kh).astype(jnp.float32) * (XATTN_DIM ** -0.5)
    p = jax.nn.softmax(logits, axis=-1).astype(vh.dtype)
    return from_heads(jnp.einsum('bhts,bhsv->bhtv', p, vh)) @ w_o


def setup_inputs(seed: int = 0) -> dict:
    key = jax.random.key(seed)
    ks = jax.random.split(key, 32)
    f32 = jnp.float32

    def nrm(k, shape, scale):
        return scale * jax.random.normal(k, shape, f32)

    def gain(k, shape):
        return 1.0 + 0.02 * jax.random.normal(k, shape, f32)

    return {
        "x": nrm(ks[0], (BATCH, SEQ, D_MODEL), 1.0),
        "mem": nrm(ks[1], (BATCH, MEM_LEN, D_MODEL), 1.0),
        "ffn_norm_pre": gain(ks[2], (DEPTH, 2, D_MODEL)),
        "ffn_norm_post": gain(ks[3], (DEPTH, 2, D_MODEL)),
        "ffn_w_gate": nrm(ks[4], (DEPTH, 2, D_MODEL, D_FF), D_MODEL ** -0.5),
        "ffn_w_up": nrm(ks[5], (DEPTH, 2, D_MODEL, D_FF), D_MODEL ** -0.5),
        "ffn_w_down": nrm(ks[6], (DEPTH, 2, D_FF, D_MODEL), D_FF ** -0.5),
        "mix_norm_pre": gain(ks[7], (DEPTH, D_MODEL)),
        "mix_norm_post": gain(ks[8], (DEPTH, D_MODEL)),
        "even_w_in": nrm(ks[9], (N_EVEN, D_MODEL, EVEN_IN), D_MODEL ** -0.5),
        "even_w_out": nrm(ks[10], (N_EVEN, MIX_WIDTH, D_MODEL), MIX_WIDTH ** -0.5),
        "hgrn_lb_logits": nrm(ks[11], (N_EVEN, HGRN_WIDTH), 0.5),
        "hgrn_norm_g": gain(ks[12], (N_EVEN, HGRN_WIDTH)),
        "diff_lambda": nrm(ks[13], (N_EVEN, 4, DIFF_QK_DIM), 0.1),
        "diff_norm_g": gain(ks[14], (N_EVEN, DIFF_WIDTH)),
        "odd_w_in": nrm(ks[15], (N_ODD, D_MODEL, ODD_IN), D_MODEL ** -0.5),
        "odd_w_out": nrm(ks[16], (N_ODD, MIX_WIDTH, D_MODEL), MIX_WIDTH ** -0.5),
        "fox_f_bias": 2.0 + nrm(ks[17], (N_ODD, FOX_HEADS), 0.1),
        "conv_w": nrm(ks[18], (N_ODD, CONV_WIDTH, CONV_CH), CONV_WIDTH ** -0.5),
        "conv_b": nrm(ks[19], (N_ODD, CONV_CH), 0.02),
        "conv_ln_g": gain(ks[20], (N_ODD, CONV_CH)),
        "conv_ln_b": nrm(ks[21], (N_ODD, CONV_CH), 0.02),
        "xattn_norm_pre": gain(ks[22], (DEPTH, D_MODEL)),
        "xattn_norm_post": gain(ks[23], (DEPTH, D_MODEL)),
        "mem_norm_g": gain(ks[24], (DEPTH, D_MODEL)),
        "xattn_w_q": nrm(ks[25], (DEPTH, D_MODEL, D_MODEL), D_MODEL ** -0.5),
        "xattn_w_kv": nrm(ks[26], (DEPTH, D_MODEL, 2 * D_MODEL), D_MODEL ** -0.5),
        "xattn_w_o": nrm(ks[27], (DEPTH, D_MODEL, D_MODEL), D_MODEL ** -0.5),
    }


def reference(x, mem, ffn_norm_pre, ffn_norm_post, ffn_w_gate, ffn_w_up, ffn_w_down,
              mix_norm_pre, mix_norm_post,
              even_w_in, even_w_out, hgrn_lb_logits, hgrn_norm_g, diff_lambda, diff_norm_g,
              odd_w_in, odd_w_out, fox_f_bias, conv_w, conv_b, conv_ln_g, conv_ln_b,
              xattn_norm_pre, xattn_norm_post, mem_norm_g, xattn_w_q, xattn_w_kv, xattn_w_o):
    lb_all = jnp.cumsum(jax.nn.softmax(hgrn_lb_logits.astype(jnp.float32), axis=0), axis=0)
    lb_all = lb_all - lb_all[0]

    def half_ffn(h, l, j):
        y = swiglu(rms_norm(h, ffn_norm_pre[l, j]), ffn_w_gate[l, j], ffn_w_up[l, j], ffn_w_down[l, j])
        return h + 0.5 * rms_norm(y, ffn_norm_post[l, j])

    h = x
    for l in range(DEPTH):
        h = half_ffn(h, l, 0)
        hn = rms_norm(h, mix_norm_pre[l])
        if l % 2 == 0:
            e = l // 2
            y = even_mixer(hn, even_w_in[e], even_w_out[e], lb_all[e], hgrn_norm_g[e],
                           diff_lambda[e], diff_norm_g[e], l)
        else:
            o = l // 2
            y = odd_mixer(hn, odd_w_in[o], odd_w_out[o], fox_f_bias[o], conv_w[o], conv_b[o],
                          conv_ln_g[o], conv_ln_b[o])
        h = h + rms_norm(y, mix_norm_post[l])
        y = cross_attention(rms_norm(h, xattn_norm_pre[l]), mem, mem_norm_g[l],
                            xattn_w_q[l], xattn_w_kv[l], xattn_w_o[l])
        h = h + rms_norm(y, xattn_norm_post[l])
        h = half_ffn(h, l, 1)
    return h
```

```python
import functools
import math

import jax
import jax.numpy as jnp
from jax import lax
from jax.experimental import pallas as pl
from jax.experimental.pallas import tpu as pltpu

F32 = jnp.float32
BF16 = jnp.bfloat16
EPS = 1e-6

HGRN_HEADS = 4
HGRN_DIM = 128
DIFF_HEADS = 4
DIFF_V_DIM = 128
DIFF_QK_DIM = 64
FOX_HEADS = 8
FOX_DIM = 64
CONV_WIDTH = 31
XATTN_HEADS = 4

LANES = 128
SUBLANES = 8
VMEM_LIMIT_BYTES = 56 * 1024 * 1024

ROW_TILE = 512
ATTN_TILE = 256
HGRN_CHUNK = 64
HGRN_ROWS = 512
CONV_ROWS = 256
CONV_HALO = 32
CONV_ROW_TILE = 32
CUM_ROWS = 256

NT_DIMS = (((1,), (1,)), ((), ()))
TN_DIMS = (((0,), (0,)), ((), ()))


def _params(*semantics):
    return pltpu.CompilerParams(dimension_semantics=semantics, vmem_limit_bytes=VMEM_LIMIT_BYTES)


def _resident(block_shape, index_map):
    return pl.BlockSpec(block_shape, index_map, pipeline_mode=pl.Buffered(1))


def _rms(x, g):
    return x * lax.rsqrt(jnp.mean(x * x, axis=-1, keepdims=True) + EPS) * g


def _dot(a, b):
    return jnp.dot(a, b, preferred_element_type=F32)


def _dot_nt(a, b):
    return lax.dot_general(a, b, NT_DIMS, preferred_element_type=F32)


def _silu(x):
    return x * jax.nn.sigmoid(x)


def _split3(x):
    hi = x.astype(BF16)
    r1 = x - hi.astype(F32)
    mid = r1.astype(BF16)
    lo = (r1 - mid.astype(F32)).astype(BF16)
    return hi, mid, lo


def _cumsum_rows(tril, x):
    hi, mid, lo = _split3(x)
    return _dot(tril, hi) + _dot(tril, mid) + _dot(tril, lo)


def _tril_ones(n):
    r = lax.broadcasted_iota(jnp.int32, (n, n), 0)
    c = lax.broadcasted_iota(jnp.int32, (n, n), 1)
    return jnp.where(r >= c, 1.0, 0.0).astype(BF16)


def _ffn_kernel(h_ref, gpre_ref, gpost_ref, wg_ref, wu_ref, wd_ref, o_ref):
    h = h_ref[...]
    xn = _rms(h, gpre_ref[...]).astype(BF16)
    g = _dot(xn, wg_ref[...])
    u = _dot(xn, wu_ref[...])
    a = (_silu(g) * u).astype(BF16)
    y = _dot(a, wd_ref[...])
    o_ref[...] = h + 0.5 * _rms(y, gpost_ref[...])


def _ffn(h, gpre, gpost, wg, wu, wd, l, j):
    n, d = h.shape
    f = wg.shape[-1]
    tm = min(ROW_TILE, n)
    sel = lambda i: (l, j, 0, 0)
    return pl.pallas_call(
        _ffn_kernel,
        grid=(n // tm,),
        in_specs=[
            pl.BlockSpec((tm, d), lambda i: (i, 0)),
            pl.BlockSpec((None, None, 1, d), sel),
            pl.BlockSpec((None, None, 1, d), sel),
            _resident((None, None, d, f), sel),
            _resident((None, None, d, f), sel),
            _resident((None, None, f, d), sel),
        ],
        out_specs=pl.BlockSpec((tm, d), lambda i: (i, 0)),
        out_shape=jax.ShapeDtypeStruct((n, d), F32),
        compiler_params=_params("parallel"),
        name="ffn_half",
    )(h, gpre, gpost, wg, wu, wd)


def _norm_proj_kernel(h_ref, g_ref, w_ref, *o_refs, splits):
    xn = _rms(h_ref[...], g_ref[...]).astype(BF16)
    for o_ref, (c0, c1) in zip(o_refs, splits):
        o_ref[...] = _dot(xn, w_ref[:, c0:c1]).astype(o_ref.dtype)


def _norm_proj(h, g, gi, w, wi, splits, dtypes, name):
    n, d = h.shape
    cols = w.shape[-1]
    tm = min(ROW_TILE, n)
    return pl.pallas_call(
        functools.partial(_norm_proj_kernel, splits=splits),
        grid=(n // tm,),
        in_specs=[
            pl.BlockSpec((tm, d), lambda i: (i, 0)),
            pl.BlockSpec((None, 1, d), lambda i: (gi, 0, 0)),
            _resident((None, d, cols), lambda i: (wi, 0, 0)),
        ],
        out_specs=[pl.BlockSpec((tm, c1 - c0), lambda i: (i, 0)) for c0, c1 in splits],
        out_shape=[jax.ShapeDtypeStruct((n, c1 - c0), dt) for (c0, c1), dt in zip(splits, dtypes)],
        compiler_params=_params("parallel"),
        name=name,
    )(h, g, w)


def _out_proj_kernel(h_ref, ya_ref, yb_ref, wa_ref, wb_ref, g_ref, o_ref):
    y = _dot(ya_ref[...], wa_ref[...]) + _dot(yb_ref[...], wb_ref[...])
    o_ref[...] = h_ref[...] + _rms(y, g_ref[...])


def _out_proj(h, ya, yb, w, g, e, l):
    n, d = h.shape
    ka, kb = ya.shape[1], yb.shape[1]
    tm = min(ROW_TILE, n)
    return pl.pallas_call(
        _out_proj_kernel,
        grid=(n // tm,),
        in_specs=[
            pl.BlockSpec((tm, d), lambda i: (i, 0)),
            pl.BlockSpec((tm, ka), lambda i: (i, 0)),
            pl.BlockSpec((tm, kb), lambda i: (i, 0)),
            _resident((None, None, ka, d), lambda i: (e, 0, 0, 0)),
            _resident((None, None, kb, d), lambda i: (e, 1, 0, 0)),
            pl.BlockSpec((None, 1, d), lambda i: (l, 0, 0)),
        ],
        out_specs=pl.BlockSpec((tm, d), lambda i: (i, 0)),
        out_shape=jax.ShapeDtypeStruct((n, d), F32),
        compiler_params=_params("parallel"),
        name="mix_out_proj",
    )(h, ya, yb, w, w, g)


def _hgrn_kernel(q_ref, f_ref, i_ref, g_ref, lbl_ref, ng_ref, o_ref, state_ref, *, e, chunk):
    rows = q_ref.shape[0]
    width = q_ref.shape[1]
    n_heads = width // HGRN_DIM

    @pl.when(pl.program_id(1) == 0)
    def _():
        state_ref[...] = jnp.zeros_like(state_ref)

    lbl = lbl_ref[...]
    ex = jnp.exp(lbl - jnp.max(lbl, axis=0, keepdims=True))
    sm = ex / jnp.sum(ex, axis=0, keepdims=True)
    lb = jnp.sum(sm[: e + 1], axis=0, keepdims=True) - sm[0:1]
    lbh = jnp.maximum(lb, 0.0)
    log_lb = jnp.log(lbh)
    log1m_lb = jnp.log1p(-lbh)
    ng = ng_ref[...]

    tril = _tril_ones(chunk)
    row_id = lax.broadcasted_iota(jnp.int32, (chunk, 1), 0)
    r2 = lax.broadcasted_iota(jnp.int32, (chunk, chunk), 0)
    c2 = lax.broadcasted_iota(jnp.int32, (chunk, chunk), 1)
    levels = []
    m = chunk // 2
    while m >= SUBLANES:
        levels.append(m)
        m //= 2
    same_block = [(r2 // (2 * m)) == (c2 // (2 * m)) for m in levels]
    second_half = [(row_id % (2 * m)) >= m for m in levels]
    sub_id = lax.broadcasted_iota(jnp.int32, (SUBLANES, 1), 0)

    def chunk_body(c, carry):
        sl = pl.ds(pl.multiple_of(c * chunk, chunk), chunk)
        z = f_ref[sl, :]
        logf = jnp.logaddexp(log_lb, log1m_lb + jax.nn.log_sigmoid(z))
        kk = (1.0 - lbh) * jax.nn.sigmoid(-z)
        q = q_ref[sl, :]
        v = i_ref[sl, :]
        b = _cumsum_rows(tril, logf)
        b_end = b[chunk - 1:chunk, :]
        qe = (q * jnp.exp(b)).astype(BF16)
        kd = (kk * jnp.exp(b_end - b)).astype(BF16)
        eb_end = jnp.exp(b_end)
        vb = v.astype(BF16)

        lvl_q, lvl_k = [], []
        for m, sec in zip(levels, second_half):
            parts = []
            for blk in range(chunk // (2 * m)):
                r = blk * 2 * m + m - 1
                parts.append(jnp.broadcast_to(b[r:r + 1, :], (2 * m, width)))
            b_mid = parts[0] if len(parts) == 1 else jnp.concatenate(parts, axis=0)
            fac = jnp.exp(-jnp.abs(b - b_mid))
            lvl_q.append(jnp.where(sec, q * fac, 0.0).astype(BF16))
            lvl_k.append(jnp.where(sec, 0.0, kk * fac).astype(BF16))

        qk = q
        for h in range(n_heads):
            hs = slice(h * HGRN_DIM, (h + 1) * HGRN_DIM)
            st = state_ref[h]
            o = _dot_nt(qe[:, hs], st.astype(BF16))
            s_off = jnp.zeros((chunk, chunk), F32)
            for lq, lk, msk in zip(lvl_q, lvl_k, same_block):
                s_off = s_off + jnp.where(msk, _dot_nt(lq[:, hs], lk[:, hs]), 0.0)
            o = o + _dot(s_off.astype(BF16), vb[:, hs])

            diag_rows = []
            for j in range(chunk // SUBLANES):
                rs = slice(j * SUBLANES, (j + 1) * SUBLANES)
                bj = b[rs, hs]
                qj = qk[rs, hs]
                kj = kk[rs, hs]
                vj = v[rs, hs]
                for t in range(SUBLANES):
                    d = jnp.where(sub_id <= t, bj[t:t + 1, :] - bj, -jnp.inf)
                    w = jnp.sum(jnp.exp(d) * (qj[t:t + 1, :] * kj), axis=-1, keepdims=True)
                    diag_rows.append(jnp.sum(w * vj, axis=0, keepdims=True))
            o = o + jnp.concatenate(diag_rows, axis=0)

            state_ref[h] = st * eb_end[:, hs] + lax.dot_general(
                vb[:, hs], kd[:, hs], TN_DIMS, preferred_element_type=F32)
            y = _rms(o, ng[:, hs]) * _silu(g_ref[sl, hs])
            o_ref[sl, hs] = y.astype(o_ref.dtype)
        return carry

    lax.fori_loop(0, rows // chunk, chunk_body, 0)


def _hgrn(proj_a, lb_logits, norm_g, e, batch, seq):
    n = proj_a.shape[0]
    width = proj_a.shape[1] // 4
    n_layers = lb_logits.shape[0]
    rows = min(HGRN_ROWS, seq)
    steps = seq // rows
    chunk = min(HGRN_CHUNK, rows)
    col = lambda k: (lambda b, i: (b * steps + i, k))
    return pl.pallas_call(
        functools.partial(_hgrn_kernel, e=e, chunk=chunk),
        grid=(batch, steps),
        in_specs=[
            pl.BlockSpec((rows, width), col(0)),
            pl.BlockSpec((rows, width), col(1)),
            pl.BlockSpec((rows, width), col(2)),
            pl.BlockSpec((rows, width), col(3)),
            pl.BlockSpec((n_layers, width), lambda b, i: (0, 0)),
            pl.BlockSpec((None, 1, width), lambda b, i: (e, 0, 0)),
        ],
        out_specs=pl.BlockSpec((rows, width), col(0)),
        out_shape=jax.ShapeDtypeStruct((n, width), BF16),
        scratch_shapes=[pltpu.VMEM((width // HGRN_DIM, HGRN_DIM, HGRN_DIM), F32)],
        compiler_params=_params("parallel", "arbitrary"),
        name="hgrn2",
    )(proj_a, proj_a, proj_a, proj_a, lb_logits, norm_g)


def _softmax_step(s, m, l, acc, v):
    m_new = jnp.maximum(m, jnp.max(s, axis=-1, keepdims=True))
    alpha = jnp.exp(m - m_new)
    p = jnp.exp(s - m_new)
    l_new = alpha * l + jnp.sum(p, axis=-1, keepdims=True)
    acc_new = alpha * acc + _dot(p.astype(BF16), v)
    return m_new, l_new, acc_new


def _diff_kernel(slope_ref, q_ref, k_ref, v_ref, lam_ref, ng_ref, o_ref, *, lam_init, tile):
    h = pl.program_id(1)
    i = pl.program_id(2)
    slope = slope_ref[h]
    lane = lax.broadcasted_iota(jnp.int32, (1, 2 * DIFF_QK_DIM), 1)
    first = lane < DIFF_QK_DIM
    q = q_ref[...] * jnp.asarray(DIFF_QK_DIM ** -0.5, BF16)
    zero = jnp.zeros_like(q)
    qs = (jnp.where(first, q, zero), jnp.where(first, zero, q))

    r2 = lax.broadcasted_iota(jnp.int32, (tile, tile), 0)
    c2 = lax.broadcasted_iota(jnp.int32, (tile, tile), 1)
    rel = slope * (c2 - r2).astype(F32)
    causal = c2 <= r2

    def block(j, carry, masked):
        ks = pl.ds(pl.multiple_of(j * tile, tile), tile)
        k = k_ref[ks, :]
        v = v_ref[ks, :]
        bias = rel + slope * ((j - i) * tile).astype(F32)
        out = []
        for mp in range(2):
            m, l, acc = carry[mp]
            s = _dot_nt(qs[mp], k) + bias
            if masked:
                s = jnp.where(causal, s, -jnp.inf)
            out.append(_softmax_step(s, m, l, acc, v))
        return tuple(out)

    init = tuple((jnp.full((tile, 1), -jnp.inf, F32), jnp.zeros((tile, 1), F32),
                  jnp.zeros((tile, DIFF_V_DIM), F32)) for _ in range(2))
    carry = lax.fori_loop(0, i, lambda j, c: block(j, c, False), init)
    (m1, l1, a1), (m2, l2, a2) = block(i, carry, True)

    lp = lam_ref[...]
    lam = (jnp.exp(jnp.sum(lp[0:1] * lp[1:2], axis=-1, keepdims=True))
           - jnp.exp(jnp.sum(lp[2:3] * lp[3:4], axis=-1, keepdims=True)) + lam_init)
    o = a1 / l1 - lam * (a2 / l2)
    o_ref[...] = (_rms(o, ng_ref[...]) * (1.0 - lam_init)).astype(o_ref.dtype)


def _diff_attn(proj_b, lam_p, norm_g, e, layer_idx, batch, seq):
    n = proj_b.shape[0]
    tile = min(ATTN_TILE, seq)
    nq = seq // tile
    lam_init = 0.8 - 0.6 * math.exp(-0.3 * layer_idx)
    slopes = jnp.asarray([2.0 ** (-8.0 * (k + 1) / DIFF_HEADS) for k in range(DIFF_HEADS)], F32)
    return pl.pallas_call(
        functools.partial(_diff_kernel, lam_init=lam_init, tile=tile),
        grid=(batch, DIFF_HEADS, nq),
        in_specs=[
            pl.BlockSpec(memory_space=pltpu.SMEM),
            pl.BlockSpec((tile, DIFF_V_DIM), lambda b, h, i: (b * nq + i, h)),
            pl.BlockSpec((seq, DIFF_V_DIM), lambda b, h, i: (b, DIFF_HEADS + h)),
            pl.BlockSpec((seq, DIFF_V_DIM), lambda b, h, i: (b, 2 * DIFF_HEADS + h)),
            pl.BlockSpec((None, 4, DIFF_QK_DIM), lambda b, h, i: (e, 0, 0)),
            pl.BlockSpec((None, 1, DIFF_V_DIM), lambda b, h, i: (e, 0, h)),
        ],
        out_specs=pl.BlockSpec((tile, DIFF_V_DIM), lambda b, h, i: (b * nq + i, h)),
        out_shape=jax.ShapeDtypeStruct((n, DIFF_HEADS * DIFF_V_DIM), BF16),
        compiler_params=_params("parallel", "parallel", "arbitrary"),
        name="diff_attention",
    )(slopes, proj_b, proj_b, proj_b, lam_p, norm_g)


def _fox_cum_kernel(f_ref, bias_ref, o_ref, carry_ref):
    @pl.when(pl.program_id(1) == 0)
    def _():
        carry_ref[...] = jnp.zeros_like(carry_ref)

    rows = f_ref.shape[0]
    logf = jax.nn.log_sigmoid(f_ref[...] + bias_ref[...])
    cs = _cumsum_rows(_tril_ones(rows), logf) + carry_ref[...]
    o_ref[...] = cs
    carry_ref[...] = cs[rows - 1:rows, :]


def _fox_cum(f_logits, f_bias, o, batch, seq):
    n, cols = f_logits.shape
    rows = min(CUM_ROWS, seq)
    steps = seq // rows
    return pl.pallas_call(
        _fox_cum_kernel,
        grid=(batch, steps),
        in_specs=[
            pl.BlockSpec((rows, cols), lambda b, i: (b * steps + i, 0)),
            pl.BlockSpec((None, 1, cols), lambda b, i: (o, 0, 0)),
        ],
        out_specs=pl.BlockSpec((rows, cols), lambda b, i: (b * steps + i, 0)),
        out_shape=jax.ShapeDtypeStruct((n, cols), F32),
        scratch_shapes=[pltpu.VMEM((1, cols), F32)],
        compiler_params=_params("parallel", "arbitrary"),
        name="fox_cumsum",
    )(f_logits, f_bias)


def _fox_kernel(q_ref, k_ref, v_ref, cfc_ref, cfr_ref, o_ref, *, tile):
    i = pl.program_id(2)
    lane = lax.broadcasted_iota(jnp.int32, (1, 2 * FOX_DIM), 1)
    first = lane < FOX_DIM
    q = q_ref[...] * jnp.asarray(FOX_DIM ** -0.5, BF16)
    zero = jnp.zeros_like(q)
    qs = (jnp.where(first, q, zero), jnp.where(first, zero, q))
    cfc = cfc_ref[...]
    r2 = lax.broadcasted_iota(jnp.int32, (tile, tile), 0)
    c2 = lax.broadcasted_iota(jnp.int32, (tile, tile), 1)
    causal = c2 <= r2

    def block(j, carry, masked):
        ks = pl.ds(pl.multiple_of(j * tile, tile), tile)
        k = k_ref[ks, :]
        v = v_ref[ks, :]
        out = []
        for hh in range(2):
            m, l, acc = carry[hh]
            s = _dot_nt(qs[hh], k) + (cfc[:, hh:hh + 1] - cfr_ref[hh:hh + 1, ks])
            if masked:
                s = jnp.where(causal, s, -jnp.inf)
            out.append(_softmax_step(s, m, l, acc, v))
        return tuple(out)

    init = tuple((jnp.full((tile, 1), -jnp.inf, F32), jnp.zeros((tile, 1), F32),
                  jnp.zeros((tile, 2 * FOX_DIM), F32)) for _ in range(2))
    carry = lax.fori_loop(0, i, lambda j, c: block(j, c, False), init)
    (m1, l1, a1), (m2, l2, a2) = block(i, carry, True)
    o_ref[...] = jnp.where(first, a1 / l1, a2 / l2).astype(o_ref.dtype)


def _fox_attn(qkv, cf_col, cf_row, batch, seq):
    n = qkv.shape[0]
    pairs = FOX_HEADS // 2
    tile = min(ATTN_TILE, seq)
    nq = seq // tile
    pw = 2 * FOX_DIM
    return pl.pallas_call(
        functools.partial(_fox_kernel, tile=tile),
        grid=(batch, pairs, nq),
        in_specs=[
            pl.BlockSpec((tile, pw), lambda b, p, i: (b * nq + i, p)),
            pl.BlockSpec((seq, pw), lambda b, p, i: (b, pairs + p)),
            pl.BlockSpec((seq, pw), lambda b, p, i: (b, 2 * pairs + p)),
            pl.BlockSpec((tile, 2), lambda b, p, i: ((b * pairs + p) * nq + i, 0)),
            pl.BlockSpec((None, 2, seq), lambda b, p, i: (b * pairs + p, 0, 0)),
        ],
        out_specs=pl.BlockSpec((tile, pw), lambda b, p, i: (b * nq + i, p)),
        out_shape=jax.ShapeDtypeStruct((n, FOX_HEADS * FOX_DIM), BF16),
        compiler_params=_params("parallel", "parallel", "arbitrary"),
        name="fox_attention",
    )(qkv, qkv, qkv, cf_col, cf_row)


def _conv_kernel(x_ref, w_ref, cb_ref, lg_ref, lb_ref, o_ref, u_ref):
    rows = x_ref.shape[0]
    ch = o_ref.shape[1]

    @pl.when(pl.program_id(1) == 0)
    def _():
        u_ref[0:CONV_HALO, :] = jnp.zeros((CONV_HALO, ch), F32)

    x = x_ref[...]
    u_ref[CONV_HALO:CONV_HALO + rows, :] = x[:, :ch] * jax.nn.sigmoid(x[:, ch:])
    first_tap = CONV_HALO - (CONV_WIDTH - 1)
    for r0 in range(0, rows, CONV_ROW_TILE):
        acc = jnp.broadcast_to(cb_ref[...], (CONV_ROW_TILE, ch))
        for j in range(CONV_WIDTH):
            start = r0 + first_tap + j
            acc = acc + w_ref[j:j + 1, :] * u_ref[start:start + CONV_ROW_TILE, :]
        mu = jnp.mean(acc, axis=-1, keepdims=True)
        cen = acc - mu
        var = jnp.mean(cen * cen, axis=-1, keepdims=True)
        y = cen * lax.rsqrt(var + EPS) * lg_ref[...] + lb_ref[...]
        o_ref[r0:r0 + CONV_ROW_TILE, :] = _silu(y).astype(o_ref.dtype)
    u_ref[0:CONV_HALO, :] = u_ref[rows:rows + CONV_HALO, :]


def _conv_module(glu_in, conv_w, conv_b, ln_g, ln_b, o, batch, seq):
    n = glu_in.shape[0]
    ch = glu_in.shape[1] // 2
    rows = min(CONV_ROWS, seq)
    steps = seq // rows
    vec = pl.BlockSpec((None, 1, ch), lambda b, i: (o, 0, 0))
    return pl.pallas_call(
        _conv_kernel,
        grid=(batch, steps),
        in_specs=[
            pl.BlockSpec((rows, 2 * ch), lambda b, i: (b * steps + i, 0)),
            pl.BlockSpec((None, CONV_WIDTH, ch), lambda b, i: (o, 0, 0)),
            vec, vec, vec,
        ],
        out_specs=pl.BlockSpec((rows, ch), lambda b, i: (b * steps + i, 0)),
        out_shape=jax.ShapeDtypeStruct((n, ch), BF16),
        scratch_shapes=[pltpu.VMEM((CONV_HALO + rows, ch), F32)],
        compiler_params=_params("parallel", "arbitrary"),
        name="conformer_conv",
    )(glu_in, conv_w, conv_b, ln_g, ln_b)


def _xattn_kernel(h_ref, gpre_ref, gpost_ref, wq_ref, kv_ref, wo_ref, o_ref):
    h = h_ref[...]
    d = h.shape[1]
    dh = d // XATTN_HEADS
    hn = _rms(h, gpre_ref[...]).astype(BF16)
    y = jnp.zeros_like(h)
    for hh in range(XATTN_HEADS):
        cs = slice(hh * dh, (hh + 1) * dh)
        q = (_dot(hn, wq_ref[:, cs]) * (dh ** -0.5)).astype(BF16)
        s = _dot_nt(q, kv_ref[:, cs])
        p = jnp.exp(s - jnp.max(s, axis=-1, keepdims=True))
        o = _dot(p.astype(BF16), kv_ref[:, d + hh * dh:d + (hh + 1) * dh])
        o = o / jnp.sum(p, axis=-1, keepdims=True)
        y = y + _dot(o.astype(BF16), wo_ref[cs, :])
    o_ref[...] = h + _rms(y, gpost_ref[...])


def _xattn(h, kv, gpre, gpost, wq, wo, l, batch, seq):
    n, d = h.shape
    mem_len = kv.shape[0] // batch
    tm = min(ROW_TILE, seq)
    steps = seq // tm
    vec = pl.BlockSpec((None, 1, d), lambda b, i: (l, 0, 0))
    return pl.pallas_call(
        _xattn_kernel,
        grid=(batch, steps),
        in_specs=[
            pl.BlockSpec((tm, d), lambda b, i: (b * steps + i, 0)),
            vec, vec,
            _resident((None, d, d), lambda b, i: (l, 0, 0)),
            pl.BlockSpec((mem_len, 2 * d), lambda b, i: (b, 0)),
            _resident((None, d, d), lambda b, i: (l, 0, 0)),
        ],
        out_specs=pl.BlockSpec((tm, d), lambda b, i: (b * steps + i, 0)),
        out_shape=jax.ShapeDtypeStruct((n, d), F32),
        compiler_params=_params("parallel", "parallel"),
        name="cross_attention",
    )(h, gpre, gpost, wq, kv, wo)


def kernel(x, mem, ffn_norm_pre, ffn_norm_post, ffn_w_gate, ffn_w_up, ffn_w_down, mix_norm_pre, mix_norm_post, even_w_in, even_w_out, hgrn_lb_logits, hgrn_norm_g, diff_lambda, diff_norm_g, odd_w_in, odd_w_out, fox_f_bias, conv_w, conv_b, conv_ln_g, conv_ln_b, xattn_norm_pre, xattn_norm_post, mem_norm_g, xattn_w_q, xattn_w_kv, xattn_w_o):
    batch, seq, d = x.shape
    depth = ffn_norm_pre.shape[0]
    mem_len = mem.shape[1]
    n = batch * seq
    row = lambda a: a[..., None, :]

    wg, wu, wd = ffn_w_gate.astype(BF16), ffn_w_up.astype(BF16), ffn_w_down.astype(BF16)
    gpre_ffn, gpost_ffn = row(ffn_norm_pre), row(ffn_norm_post)
    w_even_in = even_w_in.astype(BF16)
    hgrn_w = HGRN_HEADS * HGRN_DIM
    diff_w = DIFF_HEADS * DIFF_V_DIM
    fox_w = FOX_HEADS * FOX_DIM
    conv_ch = conv_w.shape[-1]
    mix_w = even_w_out.shape[1]
    w_even_out = even_w_out.astype(BF16).reshape(-1, 2, mix_w // 2, d)
    w_odd_out = odd_w_out.astype(BF16).reshape(-1, 2, mix_w // 2, d)
    f0 = 3 * fox_w
    w_odd_in = jnp.concatenate([
        odd_w_in[:, :, :f0],
        jnp.pad(odd_w_in[:, :, f0:f0 + FOX_HEADS], ((0, 0), (0, 0), (0, LANES - FOX_HEADS))),
        odd_w_in[:, :, f0 + FOX_HEADS:]], axis=-1).astype(BF16)
    fox_bias = row(jnp.pad(fox_f_bias, ((0, 0), (0, LANES - FOX_HEADS))))
    wq, wkv, wo = xattn_w_q.astype(BF16), xattn_w_kv.astype(BF16), xattn_w_o.astype(BF16)
    mem2 = mem.reshape(batch * mem_len, d)
    pairs = FOX_HEADS // 2

    h = x.reshape(n, d)
    for l in range(depth):
        h = _ffn(h, gpre_ffn, gpost_ffn, wg, wu, wd, l, 0)
        if l % 2 == 0:
            e = l // 2
            proj_a, proj_b = _norm_proj(
                h, row(mix_norm_pre), l, w_even_in, e,
                splits=((0, 4 * hgrn_w), (4 * hgrn_w, 4 * hgrn_w + 3 * diff_w)),
                dtypes=(F32, BF16), name="even_in_proj")
            ya = _hgrn(proj_a, hgrn_lb_logits, row(hgrn_norm_g), e, batch, seq)
            yb = _diff_attn(proj_b, diff_lambda, row(diff_norm_g), e, l, batch, seq)
            h = _out_proj(h, ya, yb, w_even_out, row(mix_norm_post), e, l)
        else:
            o = l // 2
            qkv, f_logits, glu_in = _norm_proj(
                h, row(mix_norm_pre), l, w_odd_in, o,
                splits=((0, f0), (f0, f0 + LANES), (f0 + LANES, f0 + LANES + 2 * conv_ch)),
                dtypes=(BF16, F32, F32), name="odd_in_proj")
            cf = _fox_cum(f_logits, fox_bias, o, batch, seq)[:, :FOX_HEADS]
            cf = cf.reshape(batch, seq, pairs, 2)
            cf_col = cf.transpose(0, 2, 1, 3).reshape(batch * pairs * seq, 2)
            cf_row = cf.transpose(0, 2, 3, 1).reshape(batch * pairs, 2, seq)
            yc = _fox_attn(qkv, cf_col, cf_row, batch, seq)
            yd = _conv_module(glu_in, conv_w, row(conv_b), row(conv_ln_g), row(conv_ln_b), o, batch, seq)
            h = _out_proj(h, yc, yd, w_odd_out, row(mix_norm_post), o, l)
        (kv,) = _norm_proj(mem2, row(mem_norm_g), l, wkv, l, splits=((0, 2 * d),), dtypes=(BF16,),
                           name="mem_kv_proj")
        h = _xattn(h, kv, row(xattn_norm_pre), row(xattn_norm_post), wq, wo, l, batch, seq)
        h = _ffn(h, gpre_ffn, gpost_ffn, wg, wu, wd, l, 1)
    return h.reshape(batch, seq, d)
```

```python
import functools
import math

import jax
import jax.numpy as jnp
from jax import lax
from jax.experimental import pallas as pl
from jax.experimental.pallas import tpu as pltpu

F32 = jnp.float32
BF16 = jnp.bfloat16
EPS = 1e-6

HGRN_HEADS = 4
HGRN_DIM = 128
DIFF_HEADS = 4
DIFF_V_DIM = 128
DIFF_QK_DIM = 64
FOX_HEADS = 8
FOX_DIM = 64
CONV_WIDTH = 31
XATTN_HEADS = 4

LANES = 128
SUBLANES = 8
VMEM_LIMIT_BYTES = 56 * 1024 * 1024

ROW_TILE = 512
ATTN_TILE = 256
HGRN_CHUNK = 64
HGRN_ROWS = 512
CONV_ROWS = 256
CONV_HALO = 32
CONV_ROW_TILE = 32
CUM_ROWS = 256

NT_DIMS = (((1,), (1,)), ((), ()))
TN_DIMS = (((0,), (0,)), ((), ()))


def _params(*semantics):
    return pltpu.CompilerParams(dimension_semantics=semantics, vmem_limit_bytes=VMEM_LIMIT_BYTES)


def _resident(block_shape, index_map):
    return pl.BlockSpec(block_shape, index_map, pipeline_mode=pl.Buffered(1))


def _rms(x, g):
    return x * lax.rsqrt(jnp.mean(x * x, axis=-1, keepdims=True) + EPS) * g


def _dot(a, b):
    return jnp.dot(a, b, preferred_element_type=F32)


def _dot_nt(a, b):
    return lax.dot_general(a, b, NT_DIMS, preferred_element_type=F32)


def _silu(x):
    return x * jax.nn.sigmoid(x)


def _split3(x):
    hi = x.astype(BF16)
    r1 = x - hi.astype(F32)
    mid = r1.astype(BF16)
    lo = (r1 - mid.astype(F32)).astype(BF16)
    return hi, mid, lo


def _cumsum_rows(tril, x):
    hi, mid, lo = _split3(x)
    return _dot(tril, hi) + _dot(tril, mid) + _dot(tril, lo)


def _tril_ones(n):
    r = lax.broadcasted_iota(jnp.int32, (n, n), 0)
    c = lax.broadcasted_iota(jnp.int32, (n, n), 1)
    return jnp.where(r >= c, 1.0, 0.0).astype(BF16)


def _ffn_kernel(h_ref, gpre_ref, gpost_ref, wg_ref, wu_ref, wd_ref, o_ref):
    h = h_ref[...]
    xn = _rms(h, gpre_ref[...]).astype(BF16)
    g = _dot(xn, wg_ref[...])
    u = _dot(xn, wu_ref[...])
    a = (_silu(g) * u).astype(BF16)
    y = _dot(a, wd_ref[...])
    o_ref[...] = h + 0.5 * _rms(y, gpost_ref[...])


def _ffn(h, gpre, gpost, wg, wu, wd, l, j):
    n, d = h.shape
    f = wg.shape[-1]
    tm = min(ROW_TILE, n)
    sel = lambda i: (l, j, 0, 0)
    return pl.pallas_call(
        _ffn_kernel,
        grid=(n // tm,),
        in_specs=[
            pl.BlockSpec((tm, d), lambda i: (i, 0)),
            pl.BlockSpec((None, None, 1, d), sel),
            pl.BlockSpec((None, None, 1, d), sel),
            _resident((None, None, d, f), sel),
            _resident((None, None, d, f), sel),
            _resident((None, None, f, d), sel),
        ],
        out_specs=pl.BlockSpec((tm, d), lambda i: (i, 0)),
        out_shape=jax.ShapeDtypeStruct((n, d), F32),
        compiler_params=_params("parallel"),
        name="ffn_half",
    )(h, gpre, gpost, wg, wu, wd)


def _norm_proj_kernel(h_ref, g_ref, w_ref, *o_refs, splits):
    xn = _rms(h_ref[...], g_ref[...]).astype(BF16)
    for o_ref, (c0, c1) in zip(o_refs, splits):
        o_ref[...] = _dot(xn, w_ref[:, c0:c1]).astype(o_ref.dtype)


def _norm_proj(h, g, gi, w, wi, splits, dtypes, name):
    n, d = h.shape
    cols = w.shape[-1]
    tm = min(ROW_TILE, n)
    return pl.pallas_call(
        functools.partial(_norm_proj_kernel, splits=splits),
        grid=(n // tm,),
        in_specs=[
            pl.BlockSpec((tm, d), lambda i: (i, 0)),
            pl.BlockSpec((None, 1, d), lambda i: (gi, 0, 0)),
            _resident((None, d, cols), lambda i: (wi, 0, 0)),
        ],
        out_specs=[pl.BlockSpec((tm, c1 - c0), lambda i: (i, 0)) for c0, c1 in splits],
        out_shape=[jax.ShapeDtypeStruct((n, c1 - c0), dt) for (c0, c1), dt in zip(splits, dtypes)],
        compiler_params=_params("parallel"),
        name=name,
    )(h, g, w)


def _out_proj_kernel(h_ref, ya_ref, yb_ref, wa_ref, wb_ref, g_ref, o_ref):
    y = _dot(ya_ref[...], wa_ref[...]) + _dot(yb_ref[...], wb_ref[...])
    o_ref[...] = h_ref[...] + _rms(y, g_ref[...])


def _out_proj(h, ya, yb, w, g, e, l):
    n, d = h.shape
    ka, kb = ya.shape[1], yb.shape[1]
    tm = min(ROW_TILE, n)
    return pl.pallas_call(
        _out_proj_kernel,
        grid=(n // tm,),
        in_specs=[
            pl.BlockSpec((tm, d), lambda i: (i, 0)),
            pl.BlockSpec((tm, ka), lambda i: (i, 0)),
            pl.BlockSpec((tm, kb), lambda i: (i, 0)),
            _resident((None, None, ka, d), lambda i: (e, 0, 0, 0)),
            _resident((None, None, kb, d), lambda i: (e, 1, 0, 0)),
            pl.BlockSpec((None, 1, d), lambda i: (l, 0, 0)),
        ],
        out_specs=pl.BlockSpec((tm, d), lambda i: (i, 0)),
        out_shape=jax.ShapeDtypeStruct((n, d), F32),
        compiler_params=_params("parallel"),
        name="mix_out_proj",
    )(h, ya, yb, w, w, g)


def _hgrn_kernel(q_ref, f_ref, i_ref, g_ref, lbl_ref, ng_ref, o_ref, state_ref, *, e, chunk):
    rows = q_ref.shape[0]
    width = q_ref.shape[1]
    n_heads = width // HGRN_DIM

    @pl.when(pl.program_id(1) == 0)
    def _():
        state_ref[...] = jnp.zeros_like(state_ref)

    lbl = lbl_ref[...]
    ex = jnp.exp(lbl - jnp.max(lbl, axis=0, keepdims=True))
    sm = ex / jnp.sum(ex, axis=0, keepdims=True)
    lb = jnp.sum(sm[: e + 1], axis=0, keepdims=True) - sm[0:1]
    lbh = jnp.maximum(lb, 0.0)
    log_lb = jnp.log(lbh)
    log1m_lb = jnp.log1p(-lbh)
    ng = ng_ref[...]

    tril = _tril_ones(chunk)
    row_id = lax.broadcasted_iota(jnp.int32, (chunk, 1), 0)
    r2 = lax.broadcasted_iota(jnp.int32, (chunk, chunk), 0)
    c2 = lax.broadcasted_iota(jnp.int32, (chunk, chunk), 1)
    levels = []
    m = chunk // 2
    while m >= SUBLANES:
        levels.append(m)
        m //= 2
    same_block = [(r2 // (2 * m)) == (c2 // (2 * m)) for m in levels]
    second_half = [(row_id % (2 * m)) >= m for m in levels]
    sub_id = lax.broadcasted_iota(jnp.int32, (SUBLANES, 1), 0)

    def chunk_body(c, carry):
        sl = pl.ds(pl.multiple_of(c * chunk, chunk), chunk)
        z = f_ref[sl, :]
        logf = jnp.logaddexp(log_lb, log1m_lb + jax.nn.log_sigmoid(z))
        kk = (1.0 - lbh) * jax.nn.sigmoid(-z)
        q = q_ref[sl, :]
        v = i_ref[sl, :]
        b = _cumsum_rows(tril, logf)
        b_end = b[chunk - 1:chunk, :]
        qe = (q * jnp.exp(b)).astype(BF16)
        kd = (kk * jnp.exp(b_end - b)).astype(BF16)
        eb_end = jnp.exp(b_end)
        vb = v.astype(BF16)

        lvl_q, lvl_k = [], []
        for m, sec in zip(levels, second_half):
            parts = []
            for blk in range(chunk // (2 * m)):
                r = blk * 2 * m + m - 1
                parts.append(jnp.broadcast_to(b[r:r + 1, :], (2 * m, width)))
            b_mid = parts[0] if len(parts) == 1 else jnp.concatenate(parts, axis=0)
            fac = jnp.exp(-jnp.abs(b - b_mid))
            lvl_q.append(jnp.where(sec, q * fac, 0.0).astype(BF16))
            lvl_k.append(jnp.where(sec, 0.0, kk * fac).astype(BF16))

        for h in range(n_heads):
            hs = slice(h * HGRN_DIM, (h + 1) * HGRN_DIM)
            st = state_ref[h]
            o = _dot_nt(qe[:, hs], st.astype(BF16))
            s_off = jnp.zeros((chunk, chunk), F32)
            for lq, lk, msk in zip(lvl_q, lvl_k, same_block):
                s_off = s_off + jnp.where(msk, _dot_nt(lq[:, hs], lk[:, hs]), 0.0)
            o = o + _dot(s_off.astype(BF16), vb[:, hs])

            diag = []
            for j in range(chunk // SUBLANES):
                rs = slice(j * SUBLANES, (j + 1) * SUBLANES)
                bj = b[rs, hs]
                qj = q[rs, hs]
                kj = kk[rs, hs]
                vj = v[rs, hs]
                oj = jnp.zeros((SUBLANES, HGRN_DIM), F32)
                for s in range(SUBLANES):
                    d = jnp.where(sub_id >= s, bj - bj[s:s + 1, :], -jnp.inf)
                    w = jnp.sum(jnp.exp(d) * (qj * kj[s:s + 1, :]), axis=-1, keepdims=True)
                    oj = oj + w * vj[s:s + 1, :]
                diag.append(oj)
            o = o + jnp.concatenate(diag, axis=0)

            state_ref[h] = st * eb_end[:, hs] + lax.dot_general(
                vb[:, hs], kd[:, hs], TN_DIMS, preferred_element_type=F32)
            y = _rms(o, ng[:, hs]) * _silu(g_ref[sl, hs])
            o_ref[sl, hs] = y.astype(o_ref.dtype)
        return carry

    lax.fori_loop(0, rows // chunk, chunk_body, 0)


def _hgrn(proj_a, lb_logits, norm_g, e, batch, seq):
    n = proj_a.shape[0]
    width = proj_a.shape[1] // 4
    n_layers = lb_logits.shape[0]
    rows = min(HGRN_ROWS, seq)
    steps = seq // rows
    chunk = min(HGRN_CHUNK, rows)
    col = lambda k: (lambda b, i: (b * steps + i, k))
    return pl.pallas_call(
        functools.partial(_hgrn_kernel, e=e, chunk=chunk),
        grid=(batch, steps),
        in_specs=[
            pl.BlockSpec((rows, width), col(0)),
            pl.BlockSpec((rows, width), col(1)),
            pl.BlockSpec((rows, width), col(2)),
            pl.BlockSpec((rows, width), col(3)),
            pl.BlockSpec((n_layers, width), lambda b, i: (0, 0)),
            pl.BlockSpec((None, 1, width), lambda b, i: (e, 0, 0)),
        ],
        out_specs=pl.BlockSpec((rows, width), col(0)),
        out_shape=jax.ShapeDtypeStruct((n, width), BF16),
        scratch_shapes=[pltpu.VMEM((width // HGRN_DIM, HGRN_DIM, HGRN_DIM), F32)],
        compiler_params=_params("parallel", "arbitrary"),
        name="hgrn2",
    )(proj_a, proj_a, proj_a, proj_a, lb_logits, norm_g)


def _causal_sweep(i, tile, n_groups, qst_ref, k_ref, v_ref, bias_fn, s_scr, mx_scr, ls_scr, acc_scr):
    lane_tiles = tile // LANES
    r2 = lax.broadcasted_iota(jnp.int32, (tile, tile), 0)
    c2 = lax.broadcasted_iota(jnp.int32, (tile, tile), 1)
    causal = c2 <= r2
    mx_scr[...] = jnp.full(mx_scr.shape, -jnp.inf, F32)
    ls_scr[...] = jnp.zeros(ls_scr.shape, F32)
    acc_scr[...] = jnp.zeros(acc_scr.shape, F32)

    def scores(j, masked):
        ks = pl.ds(pl.multiple_of(j * tile, tile), tile)
        for g in range(n_groups):
            gs = slice(g * LANES, (g + 1) * LANES)
            s = _dot_nt(qst_ref[g], k_ref[ks, gs])
            for half, bias in enumerate(bias_fn(g, j)):
                rs = slice(half * tile, (half + 1) * tile)
                sh = s[rs] + bias
                if masked:
                    sh = jnp.where(causal, sh, -jnp.inf)
                s_scr[g, j, rs, :] = sh
                part = sh[:, :LANES]
                for t in range(1, lane_tiles):
                    part = jnp.maximum(part, sh[:, t * LANES:(t + 1) * LANES])
                mx_scr[g, rs, :] = jnp.maximum(mx_scr[g, rs, :], part)

    def scores_body(j, carry):
        scores(j, False)
        return carry

    lax.fori_loop(0, i, scores_body, 0)
    scores(i, True)

    for g in range(n_groups):
        row_max = jnp.max(mx_scr[g], axis=-1, keepdims=True)
        mx_scr[g] = jnp.broadcast_to(row_max, mx_scr.shape[1:])

    def accumulate(j, carry):
        ks = pl.ds(pl.multiple_of(j * tile, tile), tile)
        for g in range(n_groups):
            gs = slice(g * LANES, (g + 1) * LANES)
            row_max = mx_scr[g]
            lsum = ls_scr[g]
            ps = []
            for t in range(lane_tiles):
                p = jnp.exp(s_scr[g, j, :, t * LANES:(t + 1) * LANES] - row_max)
                lsum = lsum + p
                ps.append(p.astype(BF16))
            ls_scr[g] = lsum
            p = ps[0] if lane_tiles == 1 else jnp.concatenate(ps, axis=1)
            acc_scr[g] = acc_scr[g] + _dot(p, v_ref[ks, gs])
        return carry

    lax.fori_loop(0, i + 1, accumulate, 0)


def _sweep_scratch(n_groups, n_tiles, tile):
    return [
        pltpu.VMEM((n_groups, 2 * tile, LANES), BF16),
        pltpu.VMEM((n_groups, n_tiles, 2 * tile, tile), F32),
        pltpu.VMEM((n_groups, 2 * tile, LANES), F32),
        pltpu.VMEM((n_groups, 2 * tile, LANES), F32),
        pltpu.VMEM((n_groups, 2 * tile, LANES), F32),
    ]


def _stack_half_queries(q_ref, qst_ref, n_groups, half_dim, tile):
    first = lax.broadcasted_iota(jnp.int32, (1, LANES), 1) < half_dim
    for g in range(n_groups):
        q = q_ref[:, g * LANES:(g + 1) * LANES] * jnp.asarray(half_dim ** -0.5, BF16)
        zero = jnp.zeros_like(q)
        qst_ref[g, 0:tile, :] = jnp.where(first, q, zero)
        qst_ref[g, tile:2 * tile, :] = jnp.where(first, zero, q)
    return first


def _diff_kernel(q_ref, k_ref, v_ref, lam_ref, ng_ref, o_ref, qst_ref, s_scr, mx_scr, ls_scr, acc_scr,
                 *, lam_init, tile):
    i = pl.program_id(1)
    _stack_half_queries(q_ref, qst_ref, DIFF_HEADS, DIFF_QK_DIM, tile)
    r2 = lax.broadcasted_iota(jnp.int32, (tile, tile), 0)
    c2 = lax.broadcasted_iota(jnp.int32, (tile, tile), 1)
    key_minus_query = (c2 - r2).astype(F32)

    def alibi(g, j):
        slope = 2.0 ** (-8.0 * (g + 1) / DIFF_HEADS)
        bias = slope * (key_minus_query + ((j - i) * tile).astype(F32))
        return bias, bias

    _causal_sweep(i, tile, DIFF_HEADS, qst_ref, k_ref, v_ref, alibi, s_scr, mx_scr, ls_scr, acc_scr)

    lp = lam_ref[...]
    lam = (jnp.exp(jnp.sum(lp[0:1] * lp[1:2], axis=-1, keepdims=True))
           - jnp.exp(jnp.sum(lp[2:3] * lp[3:4], axis=-1, keepdims=True)) + lam_init)
    ng = ng_ref[...]
    for g in range(DIFF_HEADS):
        gs = slice(g * LANES, (g + 1) * LANES)
        o = acc_scr[g] / jnp.sum(ls_scr[g], axis=-1, keepdims=True)
        y = o[:tile] - lam * o[tile:]
        o_ref[:, gs] = (_rms(y, ng[:, gs]) * (1.0 - lam_init)).astype(o_ref.dtype)


def _diff_attn(proj_b, lam_p, norm_g, e, layer_idx, batch, seq):
    n = proj_b.shape[0]
    width = DIFF_HEADS * DIFF_V_DIM
    tile = min(ATTN_TILE, seq)
    nq = seq // tile
    lam_init = 0.8 - 0.6 * math.exp(-0.3 * layer_idx)
    return pl.pallas_call(
        functools.partial(_diff_kernel, lam_init=lam_init, tile=tile),
        grid=(batch, nq),
        in_specs=[
            pl.BlockSpec((tile, width), lambda b, i: (b * nq + i, 0)),
            pl.BlockSpec((seq, width), lambda b, i: (b, 1)),
            pl.BlockSpec((seq, width), lambda b, i: (b, 2)),
            pl.BlockSpec((None, 4, DIFF_QK_DIM), lambda b, i: (e, 0, 0)),
            pl.BlockSpec((None, 1, width), lambda b, i: (e, 0, 0)),
        ],
        out_specs=pl.BlockSpec((tile, width), lambda b, i: (b * nq + i, 0)),
        out_shape=jax.ShapeDtypeStruct((n, width), BF16),
        scratch_shapes=_sweep_scratch(DIFF_HEADS, nq, tile),
        compiler_params=_params("parallel", "arbitrary"),
        name="diff_attention",
    )(proj_b, proj_b, proj_b, lam_p, norm_g)


def _fox_cum_kernel(f_ref, bias_ref, o_ref, carry_ref):
    @pl.when(pl.program_id(1) == 0)
    def _():
        carry_ref[...] = jnp.zeros_like(carry_ref)

    rows = f_ref.shape[0]
    logf = jax.nn.log_sigmoid(f_ref[...] + bias_ref[...])
    cs = _cumsum_rows(_tril_ones(rows), logf) + carry_ref[...]
    o_ref[...] = cs
    carry_ref[...] = cs[rows - 1:rows, :]


def _fox_cum(f_logits, f_bias, o, batch, seq):
    n, cols = f_logits.shape
    rows = min(CUM_ROWS, seq)
    steps = seq // rows
    return pl.pallas_call(
        _fox_cum_kernel,
        grid=(batch, steps),
        in_specs=[
            pl.BlockSpec((rows, cols), lambda b, i: (b * steps + i, 0)),
            pl.BlockSpec((None, 1, cols), lambda b, i: (o, 0, 0)),
        ],
        out_specs=pl.BlockSpec((rows, cols), lambda b, i: (b * steps + i, 0)),
        out_shape=jax.ShapeDtypeStruct((n, cols), F32),
        scratch_shapes=[pltpu.VMEM((1, cols), F32)],
        compiler_params=_params("parallel", "arbitrary"),
        name="fox_cumsum",
    )(f_logits, f_bias)


def _fox_kernel(q_ref, k_ref, v_ref, cfc_ref, cfr_ref, o_ref, cfb_scr, qst_ref, s_scr, mx_scr, ls_scr, acc_scr,
                *, tile):
    i = pl.program_id(1)
    pairs = FOX_HEADS // 2
    first = _stack_half_queries(q_ref, qst_ref, pairs, FOX_DIM, tile)
    cfc = cfc_ref[...]
    for h in range(FOX_HEADS):
        cfb_scr[h] = jnp.broadcast_to(cfc[:, h:h + 1], (tile, LANES))

    def forget_bias(g, j):
        cfr = cfr_ref[j]
        out = []
        for half in range(2):
            h = 2 * g + half
            cols = [cfb_scr[h] - cfr[h:h + 1, t * LANES:(t + 1) * LANES] for t in range(tile // LANES)]
            out.append(cols[0] if len(cols) == 1 else jnp.concatenate(cols, axis=1))
        return out

    _causal_sweep(i, tile, pairs, qst_ref, k_ref, v_ref, forget_bias, s_scr, mx_scr, ls_scr, acc_scr)

    for g in range(pairs):
        o = acc_scr[g] / jnp.sum(ls_scr[g], axis=-1, keepdims=True)
        o_ref[:, g * LANES:(g + 1) * LANES] = jnp.where(first, o[:tile], o[tile:]).astype(o_ref.dtype)


def _fox_attn(qkv, cf_col, cf_row, batch, seq):
    n = qkv.shape[0]
    width = FOX_HEADS * FOX_DIM
    tile = min(ATTN_TILE, seq)
    nq = seq // tile
    return pl.pallas_call(
        functools.partial(_fox_kernel, tile=tile),
        grid=(batch, nq),
        in_specs=[
            pl.BlockSpec((tile, width), lambda b, i: (b * nq + i, 0)),
            pl.BlockSpec((seq, width), lambda b, i: (b, 1)),
            pl.BlockSpec((seq, width), lambda b, i: (b, 2)),
            pl.BlockSpec((tile, LANES), lambda b, i: (b * nq + i, 0)),
            pl.BlockSpec((None, nq, FOX_HEADS, tile), lambda b, i: (b, 0, 0, 0)),
        ],
        out_specs=pl.BlockSpec((tile, width), lambda b, i: (b * nq + i, 0)),
        out_shape=jax.ShapeDtypeStruct((n, width), BF16),
        scratch_shapes=[pltpu.VMEM((FOX_HEADS, tile, LANES), F32)] + _sweep_scratch(FOX_HEADS // 2, nq, tile),
        compiler_params=_params("parallel", "arbitrary"),
        name="fox_attention",
    )(qkv, qkv, qkv, cf_col, cf_row)


def _conv_kernel(x_ref, w_ref, cb_ref, lg_ref, lb_ref, o_ref, u_ref, sh_ref, wb_ref):
    rows = x_ref.shape[0]
    ch = o_ref.shape[1]
    shifted_rows = sh_ref.shape[1]

    @pl.when(pl.program_id(1) == 0)
    def _():
        u_ref[0:CONV_HALO, :] = jnp.zeros((CONV_HALO, ch), F32)
        for j in range(CONV_WIDTH):
            wb_ref[j] = jnp.broadcast_to(w_ref[j:j + 1, :], (SUBLANES, ch))

    x = x_ref[...]
    u_ref[CONV_HALO:CONV_HALO + rows, :] = x[:, :ch] * jax.nn.sigmoid(x[:, ch:])
    for res in range(1, SUBLANES):
        sh_ref[res - 1] = u_ref[res:res + shifted_rows, :]
    first_tap = CONV_HALO - (CONV_WIDTH - 1)
    rt = CONV_ROW_TILE
    for r0 in range(0, rows, rt):
        acc = jnp.broadcast_to(cb_ref[...], (rt, ch))
        for j in range(CONV_WIDTH):
            res = (first_tap + j) % SUBLANES
            start = r0 + first_tap + j - res
            src = u_ref[start:start + rt, :] if res == 0 else sh_ref[res - 1, start:start + rt, :]
            acc = acc + jnp.concatenate([wb_ref[j]] * (rt // SUBLANES), axis=0) * src
        mu = jnp.mean(acc, axis=-1, keepdims=True)
        cen = acc - mu
        var = jnp.mean(cen * cen, axis=-1, keepdims=True)
        y = cen * lax.rsqrt(var + EPS) * lg_ref[...] + lb_ref[...]
        o_ref[r0:r0 + rt, :] = _silu(y).astype(o_ref.dtype)
    u_ref[0:CONV_HALO, :] = u_ref[rows:rows + CONV_HALO, :]


def _conv_module(glu_in, conv_w, conv_b, ln_g, ln_b, o, batch, seq):
    n = glu_in.shape[0]
    ch = glu_in.shape[1] // 2
    rows = min(CONV_ROWS, seq)
    steps = seq // rows
    vec = pl.BlockSpec((None, 1, ch), lambda b, i: (o, 0, 0))
    return pl.pallas_call(
        _conv_kernel,
        grid=(batch, steps),
        in_specs=[
            pl.BlockSpec((rows, 2 * ch), lambda b, i: (b * steps + i, 0)),
            pl.BlockSpec((None, CONV_WIDTH, ch), lambda b, i: (o, 0, 0)),
            vec, vec, vec,
        ],
        out_specs=pl.BlockSpec((rows, ch), lambda b, i: (b * steps + i, 0)),
        out_shape=jax.ShapeDtypeStruct((n, ch), BF16),
        scratch_shapes=[
            pltpu.VMEM((CONV_HALO + rows, ch), F32),
            pltpu.VMEM((SUBLANES - 1, CONV_HALO + rows - SUBLANES, ch), F32),
            pltpu.VMEM((CONV_WIDTH, SUBLANES, ch), F32),
        ],
        compiler_params=_params("parallel", "arbitrary"),
        name="conformer_conv",
    )(glu_in, conv_w, conv_b, ln_g, ln_b)


def _xattn_kernel(h_ref, gpre_ref, gpost_ref, wq_ref, kv_ref, wo_ref, o_ref):
    h = h_ref[...]
    d = h.shape[1]
    dh = d // XATTN_HEADS
    hn = _rms(h, gpre_ref[...]).astype(BF16)
    q = (_dot(hn, wq_ref[...]) * (dh ** -0.5)).astype(BF16)
    heads = []
    for hh in range(XATTN_HEADS):
        cs = slice(hh * dh, (hh + 1) * dh)
        s = _dot_nt(q[:, cs], kv_ref[:, cs])
        p = jnp.exp(s - jnp.max(s, axis=-1, keepdims=True))
        o = _dot(p.astype(BF16), kv_ref[:, d + hh * dh:d + (hh + 1) * dh])
        heads.append((o / jnp.sum(p, axis=-1, keepdims=True)).astype(BF16))
    y = _dot(jnp.concatenate(heads, axis=1), wo_ref[...])
    o_ref[...] = h + _rms(y, gpost_ref[...])


def _xattn(h, kv, gpre, gpost, wq, wo, l, batch, seq):
    n, d = h.shape
    mem_len = kv.shape[0] // batch
    tm = min(ROW_TILE, seq)
    steps = seq // tm
    vec = pl.BlockSpec((None, 1, d), lambda b, i: (l, 0, 0))
    return pl.pallas_call(
        _xattn_kernel,
        grid=(batch, steps),
        in_specs=[
            pl.BlockSpec((tm, d), lambda b, i: (b * steps + i, 0)),
            vec, vec,
            _resident((None, d, d), lambda b, i: (l, 0, 0)),
            pl.BlockSpec((mem_len, 2 * d), lambda b, i: (b, 0)),
            _resident((None, d, d), lambda b, i: (l, 0, 0)),
        ],
        out_specs=pl.BlockSpec((tm, d), lambda b, i: (b * steps + i, 0)),
        out_shape=jax.ShapeDtypeStruct((n, d), F32),
        compiler_params=_params("parallel", "parallel"),
        name="cross_attention",
    )(h, gpre, gpost, wq, kv, wo)


def kernel(x, mem, ffn_norm_pre, ffn_norm_post, ffn_w_gate, ffn_w_up, ffn_w_down, mix_norm_pre, mix_norm_post, even_w_in, even_w_out, hgrn_lb_logits, hgrn_norm_g, diff_lambda, diff_norm_g, odd_w_in, odd_w_out, fox_f_bias, conv_w, conv_b, conv_ln_g, conv_ln_b, xattn_norm_pre, xattn_norm_post, mem_norm_g, xattn_w_q, xattn_w_kv, xattn_w_o):
    batch, seq, d = x.shape
    depth = ffn_norm_pre.shape[0]
    mem_len = mem.shape[1]
    n = batch * seq
    row = lambda a: a[..., None, :]

    wg, wu, wd = ffn_w_gate.astype(BF16), ffn_w_up.astype(BF16), ffn_w_down.astype(BF16)
    gpre_ffn, gpost_ffn = row(ffn_norm_pre), row(ffn_norm_post)
    w_even_in = even_w_in.astype(BF16)
    hgrn_w = HGRN_HEADS * HGRN_DIM
    diff_w = DIFF_HEADS * DIFF_V_DIM
    fox_w = FOX_HEADS * FOX_DIM
    conv_ch = conv_w.shape[-1]
    mix_w = even_w_out.shape[1]
    w_even_out = even_w_out.astype(BF16).reshape(-1, 2, mix_w // 2, d)
    w_odd_out = odd_w_out.astype(BF16).reshape(-1, 2, mix_w // 2, d)
    f0 = 3 * fox_w
    w_odd_in = jnp.concatenate([
        odd_w_in[:, :, :f0],
        jnp.pad(odd_w_in[:, :, f0:f0 + FOX_HEADS], ((0, 0), (0, 0), (0, LANES - FOX_HEADS))),
        odd_w_in[:, :, f0 + FOX_HEADS:]], axis=-1).astype(BF16)
    fox_bias = row(jnp.pad(fox_f_bias, ((0, 0), (0, LANES - FOX_HEADS))))
    wq, wkv, wo = xattn_w_q.astype(BF16), xattn_w_kv.astype(BF16), xattn_w_o.astype(BF16)
    mem2 = mem.reshape(batch * mem_len, d)
    attn_tile = min(ATTN_TILE, seq)

    h = x.reshape(n, d)
    for l in range(depth):
        h = _ffn(h, gpre_ffn, gpost_ffn, wg, wu, wd, l, 0)
        if l % 2 == 0:
            e = l // 2
            proj_a, proj_b = _norm_proj(
                h, row(mix_norm_pre), l, w_even_in, e,
                splits=((0, 4 * hgrn_w), (4 * hgrn_w, 4 * hgrn_w + 3 * diff_w)),
                dtypes=(F32, BF16), name="even_in_proj")
            ya = _hgrn(proj_a, hgrn_lb_logits, row(hgrn_norm_g), e, batch, seq)
            yb = _diff_attn(proj_b, diff_lambda, row(diff_norm_g), e, l, batch, seq)
            h = _out_proj(h, ya, yb, w_even_out, row(mix_norm_post), e, l)
        else:
            o = l // 2
            qkv, f_logits, glu_in = _norm_proj(
                h, row(mix_norm_pre), l, w_odd_in, o,
                splits=((0, f0), (f0, f0 + LANES), (f0 + LANES, f0 + LANES + 2 * conv_ch)),
                dtypes=(BF16, F32, F32), name="odd_in_proj")
            cf_col = _fox_cum(f_logits, fox_bias, o, batch, seq)
            cf_row = cf_col[:, :FOX_HEADS].reshape(batch, seq // attn_tile, attn_tile, FOX_HEADS)
            cf_row = cf_row.transpose(0, 1, 3, 2)
            yc = _fox_attn(qkv, cf_col, cf_row, batch, seq)
            yd = _conv_module(glu_in, conv_w, row(conv_b), row(conv_ln_g), row(conv_ln_b), o, batch, seq)
            h = _out_proj(h, yc, yd, w_odd_out, row(mix_norm_post), o, l)
        (kv,) = _norm_proj(mem2, row(mem_norm_g), l, wkv, l, splits=((0, 2 * d),), dtypes=(BF16,),
                           name="mem_kv_proj")
        h = _xattn(h, kv, row(xattn_norm_pre), row(xattn_norm_post), wq, wo, l, batch, seq)
        h = _ffn(h, gpre_ffn, gpost_ffn, wg, wu, wd, l, 1)
    return h.reshape(batch, seq, d)
```

```python
import functools
import math

import jax
import jax.numpy as jnp
from jax import lax
from jax.experimental import pallas as pl
from jax.experimental.pallas import tpu as pltpu

F32 = jnp.float32
BF16 = jnp.bfloat16
EPS = 1e-6
LOG2E = math.log2(math.e)

HGRN_HEADS = 4
HGRN_DIM = 128
DIFF_HEADS = 4
DIFF_V_DIM = 128
DIFF_QK_DIM = 64
FOX_HEADS = 8
FOX_DIM = 64
CONV_WIDTH = 31
XATTN_HEADS = 4

LANES = 128
SUBLANES = 8
VMEM_LIMIT_BYTES = 56 * 1024 * 1024

ROW_TILE = 512
ATTN_TILE = 256
HGRN_CHUNK = 64
HGRN_ROWS = 512
CONV_ROWS = 256
CONV_HALO = 32
CONV_ROW_TILE = 32

NT_DIMS = (((1,), (1,)), ((), ()))
TN_DIMS = (((0,), (0,)), ((), ()))


def _params(*semantics):
    return pltpu.CompilerParams(dimension_semantics=semantics, vmem_limit_bytes=VMEM_LIMIT_BYTES)


def _resident(block_shape, index_map):
    return pl.BlockSpec(block_shape, index_map, pipeline_mode=pl.Buffered(1))


def _rms(x, g):
    return x * lax.rsqrt(jnp.mean(x * x, axis=-1, keepdims=True) + EPS) * g


def _dot(a, b):
    return jnp.dot(a, b, preferred_element_type=F32)


def _dot_nt(a, b):
    return lax.dot_general(a, b, NT_DIMS, preferred_element_type=F32)


def _silu(x):
    return x * jax.nn.sigmoid(x)


def _split3(x):
    hi = x.astype(BF16)
    r1 = x - hi.astype(F32)
    mid = r1.astype(BF16)
    lo = (r1 - mid.astype(F32)).astype(BF16)
    return hi, mid, lo


def _cumsum_rows(tril, x):
    hi, mid, lo = _split3(x)
    return _dot(tril, hi) + _dot(tril, mid) + _dot(tril, lo)


def _tril_ones(n):
    r = lax.broadcasted_iota(jnp.int32, (n, n), 0)
    c = lax.broadcasted_iota(jnp.int32, (n, n), 1)
    return jnp.where(r >= c, 1.0, 0.0).astype(BF16)


def _ffn_kernel(h_ref, gpre_ref, gpost_ref, wg_ref, wu_ref, wd_ref, o_ref):
    h = h_ref[...]
    xn = _rms(h, gpre_ref[...]).astype(BF16)
    g = _dot(xn, wg_ref[...])
    u = _dot(xn, wu_ref[...])
    a = (_silu(g) * u).astype(BF16)
    y = _dot(a, wd_ref[...])
    o_ref[...] = h + 0.5 * _rms(y, gpost_ref[...])


def _ffn(h, gpre, gpost, wg, wu, wd, l, j):
    n, d = h.shape
    f = wg.shape[-1]
    tm = min(ROW_TILE, n)
    sel = lambda i: (l, j, 0, 0)
    return pl.pallas_call(
        _ffn_kernel,
        grid=(n // tm,),
        in_specs=[
            pl.BlockSpec((tm, d), lambda i: (i, 0)),
            pl.BlockSpec((None, None, 1, d), sel),
            pl.BlockSpec((None, None, 1, d), sel),
            _resident((None, None, d, f), sel),
            _resident((None, None, d, f), sel),
            _resident((None, None, f, d), sel),
        ],
        out_specs=pl.BlockSpec((tm, d), lambda i: (i, 0)),
        out_shape=jax.ShapeDtypeStruct((n, d), F32),
        compiler_params=_params("parallel"),
        name="ffn_half",
    )(h, gpre, gpost, wg, wu, wd)


def _norm_proj_kernel(h_ref, g_ref, w_ref, *o_refs, splits):
    xn = _rms(h_ref[...], g_ref[...]).astype(BF16)
    for o_ref, (c0, c1) in zip(o_refs, splits):
        o_ref[...] = _dot(xn, w_ref[:, c0:c1]).astype(o_ref.dtype)


def _norm_proj(h, g, gi, w, wi, splits, dtypes, name):
    n, d = h.shape
    cols = w.shape[-1]
    tm = min(ROW_TILE, n)
    return pl.pallas_call(
        functools.partial(_norm_proj_kernel, splits=splits),
        grid=(n // tm,),
        in_specs=[
            pl.BlockSpec((tm, d), lambda i: (i, 0)),
            pl.BlockSpec((None, 1, d), lambda i: (gi, 0, 0)),
            _resident((None, d, cols), lambda i: (wi, 0, 0)),
        ],
        out_specs=[pl.BlockSpec((tm, c1 - c0), lambda i: (i, 0)) for c0, c1 in splits],
        out_shape=[jax.ShapeDtypeStruct((n, c1 - c0), dt) for (c0, c1), dt in zip(splits, dtypes)],
        compiler_params=_params("parallel"),
        name=name,
    )(h, g, w)


def _block_mid_rows(b, m, row_in_tile):
    chunk, width = b.shape
    if m >= SUBLANES:
        parts = [jnp.broadcast_to(b[r:r + 1, :], (2 * m, width)) for r in range(m - 1, chunk, 2 * m)]
    else:
        parts = []
        for base in range(0, chunk, SUBLANES):
            tile = None
            for off in range(0, SUBLANES, 2 * m):
                r = base + off + m - 1
                cand = jnp.broadcast_to(b[r:r + 1, :], (SUBLANES, width))
                tile = cand if tile is None else jnp.where(row_in_tile >= off, cand, tile)
            parts.append(tile)
    return parts[0] if len(parts) == 1 else jnp.concatenate(parts, axis=0)


def _hgrn_kernel(q_ref, f_ref, i_ref, g_ref, lbl_ref, ng_ref, o_ref, state_ref, *, e, chunk):
    rows = q_ref.shape[0]
    width = q_ref.shape[1]
    n_heads = width // HGRN_DIM

    @pl.when(pl.program_id(1) == 0)
    def _():
        state_ref[...] = jnp.zeros_like(state_ref)

    lbl = lbl_ref[...]
    ex = jnp.exp(lbl - jnp.max(lbl, axis=0, keepdims=True))
    sm = ex / jnp.sum(ex, axis=0, keepdims=True)
    lb = jnp.sum(sm[: e + 1], axis=0, keepdims=True) - sm[0:1]
    lbh = jnp.maximum(lb, 0.0)
    log_lb = jnp.log(lbh)
    log1m_lb = jnp.log1p(-lbh)
    ng = ng_ref[...]

    tril = _tril_ones(chunk)
    r2 = lax.broadcasted_iota(jnp.int32, (chunk, chunk), 0)
    c2 = lax.broadcasted_iota(jnp.int32, (chunk, chunk), 1)
    row_in_tile = lax.broadcasted_iota(jnp.int32, (SUBLANES, 1), 0)
    levels = []
    m = chunk // 2
    while m >= 1:
        levels.append(m)
        m //= 2
    level_pairs = [(c2 < r2) & (r2 // (2 * m) == c2 // (2 * m)) & (r2 // m != c2 // m) for m in levels]
    same_row = r2 == c2

    def chunk_body(c, carry):
        sl = pl.ds(pl.multiple_of(c * chunk, chunk), chunk)
        z = f_ref[sl, :]
        logf = jnp.logaddexp(log_lb, log1m_lb + jax.nn.log_sigmoid(z))
        kk = (1.0 - lbh) * jax.nn.sigmoid(-z)
        q = q_ref[sl, :]
        v = i_ref[sl, :]
        b = _cumsum_rows(tril, logf)
        b_end = b[chunk - 1:chunk, :]
        qe = (q * jnp.exp(b)).astype(BF16)
        kd = (kk * jnp.exp(b_end - b)).astype(BF16)
        eb_end = jnp.exp(b_end)
        vb = v.astype(BF16)
        qb = q.astype(BF16)
        kb = kk.astype(BF16)

        lvl_q, lvl_k = [], []
        for m in levels:
            if m == 1:
                lvl_q.append((q * jnp.exp(logf)).astype(BF16))
                lvl_k.append(kb)
            else:
                fac = jnp.exp(-jnp.abs(b - _block_mid_rows(b, m, row_in_tile)))
                lvl_q.append((q * fac).astype(BF16))
                lvl_k.append((kk * fac).astype(BF16))

        for h in range(n_heads):
            hs = slice(h * HGRN_DIM, (h + 1) * HGRN_DIM)
            st = state_ref[h]
            o = _dot_nt(qe[:, hs], st.astype(BF16))
            scores = jnp.where(same_row, _dot_nt(qb[:, hs], kb[:, hs]), 0.0)
            for lq, lk, pairs in zip(lvl_q, lvl_k, level_pairs):
                scores = jnp.where(pairs, _dot_nt(lq[:, hs], lk[:, hs]), scores)
            o = o + _dot(scores.astype(BF16), vb[:, hs])

            state_ref[h] = st * eb_end[:, hs] + lax.dot_general(
                vb[:, hs], kd[:, hs], TN_DIMS, preferred_element_type=F32)
            y = _rms(o, ng[:, hs]) * _silu(g_ref[sl, hs])
            o_ref[sl, hs] = y.astype(o_ref.dtype)
        return carry

    lax.fori_loop(0, rows // chunk, chunk_body, 0, unroll=2)


def _hgrn(proj_a, lb_logits, norm_g, e, batch, seq):
    n = proj_a.shape[0]
    width = proj_a.shape[1] // 4
    n_layers = lb_logits.shape[0]
    rows = min(HGRN_ROWS, seq)
    steps = seq // rows
    chunk = min(HGRN_CHUNK, rows)
    col = lambda k: (lambda b, i: (b * steps + i, k))
    return pl.pallas_call(
        functools.partial(_hgrn_kernel, e=e, chunk=chunk),
        grid=(batch, steps),
        in_specs=[
            pl.BlockSpec((rows, width), col(0)),
            pl.BlockSpec((rows, width), col(1)),
            pl.BlockSpec((rows, width), col(2)),
            pl.BlockSpec((rows, width), col(3)),
            pl.BlockSpec((n_layers, width), lambda b, i: (0, 0)),
            pl.BlockSpec((None, 1, width), lambda b, i: (e, 0, 0)),
        ],
        out_specs=pl.BlockSpec((rows, width), col(0)),
        out_shape=jax.ShapeDtypeStruct((n, width), BF16),
        scratch_shapes=[pltpu.VMEM((width // HGRN_DIM, HGRN_DIM, HGRN_DIM), F32)],
        compiler_params=_params("parallel", "arbitrary"),
        name="hgrn2",
    )(proj_a, proj_a, proj_a, proj_a, lb_logits, norm_g)


def _causal_sweep(i, tile, n_groups, qst_ref, k_ref, v_ref, bias_fn, s_scr, mx_scr, ls_scr, acc_scr):
    lane_tiles = tile // LANES
    r2 = lax.broadcasted_iota(jnp.int32, (tile, tile), 0)
    c2 = lax.broadcasted_iota(jnp.int32, (tile, tile), 1)
    causal = c2 <= r2
    mx_scr[...] = jnp.full(mx_scr.shape, -jnp.inf, F32)
    ls_scr[...] = jnp.zeros(ls_scr.shape, F32)
    acc_scr[...] = jnp.zeros(acc_scr.shape, F32)

    def scores(j, masked):
        ks = pl.ds(pl.multiple_of(j * tile, tile), tile)
        for g in range(n_groups):
            gs = slice(g * LANES, (g + 1) * LANES)
            s = _dot_nt(qst_ref[g], k_ref[ks, gs])
            for half, bias in enumerate(bias_fn(g, j)):
                rs = slice(half * tile, (half + 1) * tile)
                sh = s[rs] * LOG2E + bias
                if masked:
                    sh = jnp.where(causal, sh, -jnp.inf)
                s_scr[g, j, rs, :] = sh
                part = sh[:, :LANES]
                for t in range(1, lane_tiles):
                    part = jnp.maximum(part, sh[:, t * LANES:(t + 1) * LANES])
                mx_scr[g, rs, :] = jnp.maximum(mx_scr[g, rs, :], part)

    def scores_body(j, carry):
        scores(j, False)
        return carry

    lax.fori_loop(0, i, scores_body, 0)
    scores(i, True)

    for g in range(n_groups):
        row_max = jnp.max(mx_scr[g], axis=-1, keepdims=True)
        mx_scr[g] = jnp.broadcast_to(row_max, mx_scr.shape[1:])

    def accumulate(j, carry):
        ks = pl.ds(pl.multiple_of(j * tile, tile), tile)
        for g in range(n_groups):
            gs = slice(g * LANES, (g + 1) * LANES)
            row_max = mx_scr[g]
            lsum = ls_scr[g]
            ps = []
            for t in range(lane_tiles):
                p = jnp.exp2(s_scr[g, j, :, t * LANES:(t + 1) * LANES] - row_max)
                lsum = lsum + p
                ps.append(p.astype(BF16))
            ls_scr[g] = lsum
            p = ps[0] if lane_tiles == 1 else jnp.concatenate(ps, axis=1)
            acc_scr[g] = acc_scr[g] + _dot(p, v_ref[ks, gs])
        return carry

    lax.fori_loop(0, i + 1, accumulate, 0)


def _sweep_scratch(n_groups, n_tiles, tile):
    return [
        pltpu.VMEM((n_groups, 2 * tile, LANES), BF16),
        pltpu.VMEM((n_groups, n_tiles, 2 * tile, tile), F32),
        pltpu.VMEM((n_groups, 2 * tile, LANES), F32),
        pltpu.VMEM((n_groups, 2 * tile, LANES), F32),
        pltpu.VMEM((n_groups, 2 * tile, LANES), F32),
    ]


def _stack_half_queries(q_ref, qst_ref, n_groups, half_dim, tile):
    first = lax.broadcasted_iota(jnp.int32, (1, LANES), 1) < half_dim
    for g in range(n_groups):
        q = q_ref[:, g * LANES:(g + 1) * LANES] * jnp.asarray(half_dim ** -0.5, BF16)
        zero = jnp.zeros_like(q)
        qst_ref[g, 0:tile, :] = jnp.where(first, q, zero)
        qst_ref[g, tile:2 * tile, :] = jnp.where(first, zero, q)
    return first


def _diff_kernel(q_ref, k_ref, v_ref, lam_ref, ng_ref, o_ref, qst_ref, s_scr, mx_scr, ls_scr, acc_scr,
                 *, lam_init, tile):
    i = pl.program_id(1)
    _stack_half_queries(q_ref, qst_ref, DIFF_HEADS, DIFF_QK_DIM, tile)
    r2 = lax.broadcasted_iota(jnp.int32, (tile, tile), 0)
    c2 = lax.broadcasted_iota(jnp.int32, (tile, tile), 1)
    key_minus_query = (c2 - r2).astype(F32)

    def alibi(g, j):
        slope = 2.0 ** (-8.0 * (g + 1) / DIFF_HEADS)
        bias = (slope * LOG2E) * (key_minus_query + ((j - i) * tile).astype(F32))
        return bias, bias

    _causal_sweep(i, tile, DIFF_HEADS, qst_ref, k_ref, v_ref, alibi, s_scr, mx_scr, ls_scr, acc_scr)

    lp = lam_ref[...]
    lam = (jnp.exp(jnp.sum(lp[0:1] * lp[1:2], axis=-1, keepdims=True))
           - jnp.exp(jnp.sum(lp[2:3] * lp[3:4], axis=-1, keepdims=True)) + lam_init)
    ng = ng_ref[...]
    for g in range(DIFF_HEADS):
        gs = slice(g * LANES, (g + 1) * LANES)
        o = acc_scr[g] / jnp.sum(ls_scr[g], axis=-1, keepdims=True)
        y = o[:tile] - lam * o[tile:]
        o_ref[:, gs] = (_rms(y, ng[:, gs]) * (1.0 - lam_init)).astype(o_ref.dtype)


def _diff_attn(proj_b, lam_p, norm_g, e, layer_idx, batch, seq):
    n = proj_b.shape[0]
    width = DIFF_HEADS * DIFF_V_DIM
    tile = min(ATTN_TILE, seq)
    nq = seq // tile
    lam_init = 0.8 - 0.6 * math.exp(-0.3 * layer_idx)
    return pl.pallas_call(
        functools.partial(_diff_kernel, lam_init=lam_init, tile=tile),
        grid=(batch, nq),
        in_specs=[
            pl.BlockSpec((tile, width), lambda b, i: (b * nq + i, 0)),
            pl.BlockSpec((seq, width), lambda b, i: (b, 1)),
            pl.BlockSpec((seq, width), lambda b, i: (b, 2)),
            pl.BlockSpec((None, 4, DIFF_QK_DIM), lambda b, i: (e, 0, 0)),
            pl.BlockSpec((None, 1, width), lambda b, i: (e, 0, 0)),
        ],
        out_specs=pl.BlockSpec((tile, width), lambda b, i: (b * nq + i, 0)),
        out_shape=jax.ShapeDtypeStruct((n, width), BF16),
        scratch_shapes=_sweep_scratch(DIFF_HEADS, nq, tile),
        compiler_params=_params("parallel", "arbitrary"),
        name="diff_attention",
    )(proj_b, proj_b, proj_b, lam_p, norm_g)


def _odd_proj_kernel(h_ref, g_ref, w_ref, fb_ref, qkv_ref, cf_ref, glu_ref, carry_ref, *, splits, steps_per_seq):
    @pl.when(pl.program_id(0) % steps_per_seq == 0)
    def _():
        carry_ref[...] = jnp.zeros_like(carry_ref)

    rows = h_ref.shape[0]
    xn = _rms(h_ref[...], g_ref[...]).astype(BF16)
    (q0, q1), (f0, f1), (c0, c1) = splits
    qkv_ref[...] = _dot(xn, w_ref[:, q0:q1]).astype(qkv_ref.dtype)
    glu_ref[...] = _dot(xn, w_ref[:, c0:c1])
    logf = jax.nn.log_sigmoid(_dot(xn, w_ref[:, f0:f1]) + fb_ref[...])
    tril = _tril_ones(LANES)
    total = carry_ref[...]
    for r0 in range(0, rows, LANES):
        cs = _cumsum_rows(tril, logf[r0:r0 + LANES]) + total
        cf_ref[r0:r0 + LANES, :] = cs
        total = cs[LANES - 1:LANES, :]
    carry_ref[...] = total


def _odd_proj(h, g, gi, w, wi, f_bias, splits, seq):
    n, d = h.shape
    cols = w.shape[-1]
    tm = min(ROW_TILE, seq)
    widths = [c1 - c0 for c0, c1 in splits]
    return pl.pallas_call(
        functools.partial(_odd_proj_kernel, splits=splits, steps_per_seq=seq // tm),
        grid=(n // tm,),
        in_specs=[
            pl.BlockSpec((tm, d), lambda i: (i, 0)),
            pl.BlockSpec((None, 1, d), lambda i: (gi, 0, 0)),
            _resident((None, d, cols), lambda i: (wi, 0, 0)),
            pl.BlockSpec((None, 1, widths[1]), lambda i: (wi, 0, 0)),
        ],
        out_specs=[pl.BlockSpec((tm, wd), lambda i: (i, 0)) for wd in widths],
        out_shape=[jax.ShapeDtypeStruct((n, wd), dt) for wd, dt in zip(widths, (BF16, F32, F32))],
        scratch_shapes=[pltpu.VMEM((1, widths[1]), F32)],
        compiler_params=_params("arbitrary"),
        name="odd_in_proj",
    )(h, g, w, f_bias)


def _fox_kernel(q_ref, k_ref, v_ref, cfc_ref, cfr_ref, o_ref, cfb_scr, qst_ref, s_scr, mx_scr, ls_scr, acc_scr,
                *, tile):
    i = pl.program_id(1)
    pairs = FOX_HEADS // 2
    first = _stack_half_queries(q_ref, qst_ref, pairs, FOX_DIM, tile)
    cfc = cfc_ref[...]
    for h in range(FOX_HEADS):
        cfb_scr[h] = jnp.broadcast_to(cfc[:, h:h + 1] * LOG2E, (tile, LANES))

    def forget_bias(g, j):
        cfr = cfr_ref[j] * LOG2E
        out = []
        for half in range(2):
            h = 2 * g + half
            cols = [cfb_scr[h] - cfr[h:h + 1, t * LANES:(t + 1) * LANES] for t in range(tile // LANES)]
            out.append(cols[0] if len(cols) == 1 else jnp.concatenate(cols, axis=1))
        return out

    _causal_sweep(i, tile, pairs, qst_ref, k_ref, v_ref, forget_bias, s_scr, mx_scr, ls_scr, acc_scr)

    for g in range(pairs):
        o = acc_scr[g] / jnp.sum(ls_scr[g], axis=-1, keepdims=True)
        o_ref[:, g * LANES:(g + 1) * LANES] = jnp.where(first, o[:tile], o[tile:]).astype(o_ref.dtype)


def _fox_attn(qkv, cf_col, cf_row, batch, seq):
    n = qkv.shape[0]
    width = FOX_HEADS * FOX_DIM
    tile = min(ATTN_TILE, seq)
    nq = seq // tile
    return pl.pallas_call(
        functools.partial(_fox_kernel, tile=tile),
        grid=(batch, nq),
        in_specs=[
            pl.BlockSpec((tile, width), lambda b, i: (b * nq + i, 0)),
            pl.BlockSpec((seq, width), lambda b, i: (b, 1)),
            pl.BlockSpec((seq, width), lambda b, i: (b, 2)),
            pl.BlockSpec((tile, LANES), lambda b, i: (b * nq + i, 0)),
            pl.BlockSpec((None, nq, FOX_HEADS, tile), lambda b, i: (b, 0, 0, 0)),
        ],
        out_specs=pl.BlockSpec((tile, width), lambda b, i: (b * nq + i, 0)),
        out_shape=jax.ShapeDtypeStruct((n, width), BF16),
        scratch_shapes=[pltpu.VMEM((FOX_HEADS, tile, LANES), F32)] + _sweep_scratch(FOX_HEADS // 2, nq, tile),
        compiler_params=_params("parallel", "arbitrary"),
        name="fox_attention",
    )(qkv, qkv, qkv, cf_col, cf_row)


def _conv_kernel(x_ref, w_ref, cb_ref, lg_ref, lb_ref, o_ref, u_ref, sh_ref, wb_ref):
    rows = x_ref.shape[0]
    ch = o_ref.shape[1]
    shifted_rows = sh_ref.shape[1]

    @pl.when(pl.program_id(1) == 0)
    def _():
        u_ref[0:CONV_HALO, :] = jnp.zeros((CONV_HALO, ch), F32)
        for j in range(CONV_WIDTH):
            wb_ref[j] = jnp.broadcast_to(w_ref[j:j + 1, :], (SUBLANES, ch))

    x = x_ref[...]
    u_ref[CONV_HALO:CONV_HALO + rows, :] = x[:, :ch] * jax.nn.sigmoid(x[:, ch:])
    for res in range(1, SUBLANES):
        sh_ref[res - 1] = u_ref[res:res + shifted_rows, :]
    first_tap = CONV_HALO - (CONV_WIDTH - 1)
    rt = CONV_ROW_TILE
    for r0 in range(0, rows, rt):
        acc = jnp.broadcast_to(cb_ref[...], (rt, ch))
        for j in range(CONV_WIDTH):
            res = (first_tap + j) % SUBLANES
            start = r0 + first_tap + j - res
            src = u_ref[start:start + rt, :] if res == 0 else sh_ref[res - 1, start:start + rt, :]
            acc = acc + jnp.concatenate([wb_ref[j]] * (rt // SUBLANES), axis=0) * src
        mu = jnp.mean(acc, axis=-1, keepdims=True)
        cen = acc - mu
        var = jnp.mean(cen * cen, axis=-1, keepdims=True)
        y = cen * lax.rsqrt(var + EPS) * lg_ref[...] + lb_ref[...]
        o_ref[r0:r0 + rt, :] = _silu(y).astype(o_ref.dtype)
    u_ref[0:CONV_HALO, :] = u_ref[rows:rows + CONV_HALO, :]


def _conv_module(glu_in, conv_w, conv_b, ln_g, ln_b, o, batch, seq):
    n = glu_in.shape[0]
    ch = glu_in.shape[1] // 2
    rows = min(CONV_ROWS, seq)
    steps = seq // rows
    vec = pl.BlockSpec((None, 1, ch), lambda b, i: (o, 0, 0))
    return pl.pallas_call(
        _conv_kernel,
        grid=(batch, steps),
        in_specs=[
            pl.BlockSpec((rows, 2 * ch), lambda b, i: (b * steps + i, 0)),
            pl.BlockSpec((None, CONV_WIDTH, ch), lambda b, i: (o, 0, 0)),
            vec, vec, vec,
        ],
        out_specs=pl.BlockSpec((rows, ch), lambda b, i: (b * steps + i, 0)),
        out_shape=jax.ShapeDtypeStruct((n, ch), BF16),
        scratch_shapes=[
            pltpu.VMEM((CONV_HALO + rows, ch), F32),
            pltpu.VMEM((SUBLANES - 1, CONV_HALO + rows - SUBLANES, ch), F32),
            pltpu.VMEM((CONV_WIDTH, SUBLANES, ch), F32),
        ],
        compiler_params=_params("parallel", "arbitrary"),
        name="conformer_conv",
    )(glu_in, conv_w, conv_b, ln_g, ln_b)


def _xattn_kernel(h_ref, ya_ref, yb_ref, wa_ref, wb_ref, gmix_ref, gpre_ref, gpost_ref, wq_ref, kv_ref, wo_ref,
                  o_ref):
    y_mix = _dot(ya_ref[...], wa_ref[...]) + _dot(yb_ref[...], wb_ref[...])
    h = h_ref[...] + _rms(y_mix, gmix_ref[...])
    d = h.shape[1]
    dh = d // XATTN_HEADS
    hn = _rms(h, gpre_ref[...]).astype(BF16)
    q = (_dot(hn, wq_ref[...]) * (dh ** -0.5)).astype(BF16)
    heads = []
    for hh in range(XATTN_HEADS):
        cs = slice(hh * dh, (hh + 1) * dh)
        s = _dot_nt(q[:, cs], kv_ref[:, cs])
        p = jnp.exp(s - jnp.max(s, axis=-1, keepdims=True))
        o = _dot(p.astype(BF16), kv_ref[:, d + hh * dh:d + (hh + 1) * dh])
        heads.append((o / jnp.sum(p, axis=-1, keepdims=True)).astype(BF16))
    y = _dot(jnp.concatenate(heads, axis=1), wo_ref[...])
    o_ref[...] = h + _rms(y, gpost_ref[...])


def _mix_out_xattn(h, ya, yb, w_out, e, gmix, kv, gpre, gpost, wq, wo, l, batch, seq):
    n, d = h.shape
    ka, kb = ya.shape[1], yb.shape[1]
    mem_len = kv.shape[0] // batch
    tm = min(ROW_TILE, seq)
    steps = seq // tm
    rows = lambda width: pl.BlockSpec((tm, width), lambda b, i: (b * steps + i, 0))
    vec = pl.BlockSpec((None, 1, d), lambda b, i: (l, 0, 0))
    return pl.pallas_call(
        _xattn_kernel,
        grid=(batch, steps),
        in_specs=[
            rows(d), rows(ka), rows(kb),
            _resident((None, None, ka, d), lambda b, i: (e, 0, 0, 0)),
            _resident((None, None, kb, d), lambda b, i: (e, 1, 0, 0)),
            vec, vec, vec,
            _resident((None, d, d), lambda b, i: (l, 0, 0)),
            pl.BlockSpec((mem_len, 2 * d), lambda b, i: (b, 0)),
            _resident((None, d, d), lambda b, i: (l, 0, 0)),
        ],
        out_specs=rows(d),
        out_shape=jax.ShapeDtypeStruct((n, d), F32),
        compiler_params=_params("parallel", "parallel"),
        name="mix_out_cross_attention",
    )(h, ya, yb, w_out, w_out, gmix, gpre, gpost, wq, kv, wo)


def kernel(x, mem, ffn_norm_pre, ffn_norm_post, ffn_w_gate, ffn_w_up, ffn_w_down, mix_norm_pre, mix_norm_post, even_w_in, even_w_out, hgrn_lb_logits, hgrn_norm_g, diff_lambda, diff_norm_g, odd_w_in, odd_w_out, fox_f_bias, conv_w, conv_b, conv_ln_g, conv_ln_b, xattn_norm_pre, xattn_norm_post, mem_norm_g, xattn_w_q, xattn_w_kv, xattn_w_o):
    batch, seq, d = x.shape
    depth = ffn_norm_pre.shape[0]
    mem_len = mem.shape[1]
    n = batch * seq
    row = lambda a: a[..., None, :]

    wg, wu, wd = ffn_w_gate.astype(BF16), ffn_w_up.astype(BF16), ffn_w_down.astype(BF16)
    gpre_ffn, gpost_ffn = row(ffn_norm_pre), row(ffn_norm_post)
    w_even_in = even_w_in.astype(BF16)
    hgrn_w = HGRN_HEADS * HGRN_DIM
    diff_w = DIFF_HEADS * DIFF_V_DIM
    fox_w = FOX_HEADS * FOX_DIM
    conv_ch = conv_w.shape[-1]
    mix_w = even_w_out.shape[1]
    w_even_out = even_w_out.astype(BF16).reshape(-1, 2, mix_w // 2, d)
    w_odd_out = odd_w_out.astype(BF16).reshape(-1, 2, mix_w // 2, d)
    f0 = 3 * fox_w
    w_odd_in = jnp.concatenate([
        odd_w_in[:, :, :f0],
        jnp.pad(odd_w_in[:, :, f0:f0 + FOX_HEADS], ((0, 0), (0, 0), (0, LANES - FOX_HEADS))),
        odd_w_in[:, :, f0 + FOX_HEADS:]], axis=-1).astype(BF16)
    fox_bias = row(jnp.pad(fox_f_bias, ((0, 0), (0, LANES - FOX_HEADS))))
    wq, wkv, wo = xattn_w_q.astype(BF16), xattn_w_kv.astype(BF16), xattn_w_o.astype(BF16)
    mem2 = mem.reshape(batch * mem_len, d)
    attn_tile = min(ATTN_TILE, seq)

    h = x.reshape(n, d)
    for l in range(depth):
        h = _ffn(h, gpre_ffn, gpost_ffn, wg, wu, wd, l, 0)
        if l % 2 == 0:
            e = l // 2
            proj_a, proj_b = _norm_proj(
                h, row(mix_norm_pre), l, w_even_in, e,
                splits=((0, 4 * hgrn_w), (4 * hgrn_w, 4 * hgrn_w + 3 * diff_w)),
                dtypes=(F32, BF16), name="even_in_proj")
            ya = _hgrn(proj_a, hgrn_lb_logits, row(hgrn_norm_g), e, batch, seq)
            yb = _diff_attn(proj_b, diff_lambda, row(diff_norm_g), e, l, batch, seq)
            w_mix_out, mix_idx = w_even_out, e
        else:
            o = l // 2
            qkv, cf_col, glu_in = _odd_proj(
                h, row(mix_norm_pre), l, w_odd_in, o, fox_bias,
                splits=((0, f0), (f0, f0 + LANES), (f0 + LANES, f0 + LANES + 2 * conv_ch)), seq=seq)
            cf_row = cf_col[:, :FOX_HEADS].reshape(batch, seq // attn_tile, attn_tile, FOX_HEADS)
            cf_row = cf_row.transpose(0, 1, 3, 2)
            ya = _fox_attn(qkv, cf_col, cf_row, batch, seq)
            yb = _conv_module(glu_in, conv_w, row(conv_b), row(conv_ln_g), row(conv_ln_b), o, batch, seq)
            w_mix_out, mix_idx = w_odd_out, o
        (kv,) = _norm_proj(mem2, row(mem_norm_g), l, wkv, l, splits=((0, 2 * d),), dtypes=(BF16,),
                           name="mem_kv_proj")
        h = _mix_out_xattn(h, ya, yb, w_mix_out, mix_idx, row(mix_norm_post), kv, row(xattn_norm_pre),
                           row(xattn_norm_post), wq, wo, l, batch, seq)
        h = _ffn(h, gpre_ffn, gpost_ffn, wg, wu, wd, l, 1)
    return h.reshape(batch, seq, d)
```

```python
import functools
import math

import jax
import jax.numpy as jnp
from jax import lax
from jax.experimental import pallas as pl
from jax.experimental.pallas import tpu as pltpu

F32 = jnp.float32
BF16 = jnp.bfloat16
EPS = 1e-6
LOG2E = math.log2(math.e)

HGRN_HEADS = 4
HGRN_DIM = 128
DIFF_HEADS = 4
DIFF_V_DIM = 128
DIFF_QK_DIM = 64
FOX_HEADS = 8
FOX_DIM = 64
CONV_WIDTH = 31
XATTN_HEADS = 4

LANES = 128
SUBLANES = 8
VMEM_LIMIT_BYTES = 56 * 1024 * 1024

ROW_TILE = 512
ATTN_TILE = 256
HGRN_CHUNK = 64
HGRN_ROWS = 512
CONV_ROWS = 256
CONV_HALO = 32
CONV_ROW_TILE = 32

NT_DIMS = (((1,), (1,)), ((), ()))
TN_DIMS = (((0,), (0,)), ((), ()))


def _params(*semantics):
    return pltpu.CompilerParams(dimension_semantics=semantics, vmem_limit_bytes=VMEM_LIMIT_BYTES)


def _resident(block_shape, index_map):
    return pl.BlockSpec(block_shape, index_map, pipeline_mode=pl.Buffered(1))


def _rms(x, g):
    return x * lax.rsqrt(jnp.mean(x * x, axis=-1, keepdims=True) + EPS) * g


def _dot(a, b):
    return jnp.dot(a, b, preferred_element_type=F32)


def _dot_nt(a, b):
    return lax.dot_general(a, b, NT_DIMS, preferred_element_type=F32)


def _silu(x):
    return x * jax.nn.sigmoid(x)


def _split3(x):
    hi = x.astype(BF16)
    r1 = x - hi.astype(F32)
    mid = r1.astype(BF16)
    lo = (r1 - mid.astype(F32)).astype(BF16)
    return hi, mid, lo


def _cumsum_rows(tril, x):
    hi, mid, lo = _split3(x)
    return _dot(tril, hi) + _dot(tril, mid) + _dot(tril, lo)


def _tril_ones(n):
    r = lax.broadcasted_iota(jnp.int32, (n, n), 0)
    c = lax.broadcasted_iota(jnp.int32, (n, n), 1)
    return jnp.where(r >= c, 1.0, 0.0).astype(BF16)


def _ffn_kernel(h_ref, gpre_ref, gpost_ref, wg_ref, wu_ref, wd_ref, o_ref):
    h = h_ref[...]
    xn = _rms(h, gpre_ref[...]).astype(BF16)
    g = _dot(xn, wg_ref[...])
    u = _dot(xn, wu_ref[...])
    a = (_silu(g) * u).astype(BF16)
    y = _dot(a, wd_ref[...])
    o_ref[...] = h + 0.5 * _rms(y, gpost_ref[...])


def _ffn(h, gpre, gpost, wg, wu, wd, l, j):
    n, d = h.shape
    f = wg.shape[-1]
    tm = min(ROW_TILE, n)
    sel = lambda i: (l, j, 0, 0)
    return pl.pallas_call(
        _ffn_kernel,
        grid=(n // tm,),
        in_specs=[
            pl.BlockSpec((tm, d), lambda i: (i, 0)),
            pl.BlockSpec((None, None, 1, d), sel),
            pl.BlockSpec((None, None, 1, d), sel),
            _resident((None, None, d, f), sel),
            _resident((None, None, d, f), sel),
            _resident((None, None, f, d), sel),
        ],
        out_specs=pl.BlockSpec((tm, d), lambda i: (i, 0)),
        out_shape=jax.ShapeDtypeStruct((n, d), F32),
        compiler_params=_params("parallel"),
        name="ffn_half",
    )(h, gpre, gpost, wg, wu, wd)


def _norm_proj_kernel(h_ref, g_ref, w_ref, *o_refs, splits):
    xn = _rms(h_ref[...], g_ref[...]).astype(BF16)
    for o_ref, (c0, c1) in zip(o_refs, splits):
        o_ref[...] = _dot(xn, w_ref[:, c0:c1]).astype(o_ref.dtype)


def _norm_proj(h, g, gi, w, wi, splits, dtypes, name):
    n, d = h.shape
    cols = w.shape[-1]
    tm = min(ROW_TILE, n)
    return pl.pallas_call(
        functools.partial(_norm_proj_kernel, splits=splits),
        grid=(n // tm,),
        in_specs=[
            pl.BlockSpec((tm, d), lambda i: (i, 0)),
            pl.BlockSpec((None, 1, d), lambda i: (gi, 0, 0)),
            _resident((None, d, cols), lambda i: (wi, 0, 0)),
        ],
        out_specs=[pl.BlockSpec((tm, c1 - c0), lambda i: (i, 0)) for c0, c1 in splits],
        out_shape=[jax.ShapeDtypeStruct((n, c1 - c0), dt) for (c0, c1), dt in zip(splits, dtypes)],
        compiler_params=_params("parallel"),
        name=name,
    )(h, g, w)


def _block_mid_rows(b, m, row_in_tile):
    chunk, width = b.shape
    if m >= SUBLANES:
        parts = [jnp.broadcast_to(b[r:r + 1, :], (2 * m, width)) for r in range(m - 1, chunk, 2 * m)]
    else:
        parts = []
        for base in range(0, chunk, SUBLANES):
            tile = None
            for off in range(0, SUBLANES, 2 * m):
                r = base + off + m - 1
                cand = jnp.broadcast_to(b[r:r + 1, :], (SUBLANES, width))
                tile = cand if tile is None else jnp.where(row_in_tile >= off, cand, tile)
            parts.append(tile)
    return parts[0] if len(parts) == 1 else jnp.concatenate(parts, axis=0)


def _hgrn_kernel(q_ref, f_ref, i_ref, g_ref, lbl_ref, ng_ref, o_ref, state_ref, *, e, chunk):
    rows = q_ref.shape[0]
    width = q_ref.shape[1]
    n_heads = width // HGRN_DIM

    @pl.when(pl.program_id(1) == 0)
    def _():
        state_ref[...] = jnp.zeros_like(state_ref)

    lbl = lbl_ref[...]
    ex = jnp.exp(lbl - jnp.max(lbl, axis=0, keepdims=True))
    sm = ex / jnp.sum(ex, axis=0, keepdims=True)
    lb = jnp.sum(sm[: e + 1], axis=0, keepdims=True) - sm[0:1]
    lbh = jnp.maximum(lb, 0.0)
    log_lb = jnp.log(lbh)
    log1m_lb = jnp.log1p(-lbh)
    ng = ng_ref[...]

    tril = _tril_ones(chunk)
    r2 = lax.broadcasted_iota(jnp.int32, (chunk, chunk), 0)
    c2 = lax.broadcasted_iota(jnp.int32, (chunk, chunk), 1)
    row_in_tile = lax.broadcasted_iota(jnp.int32, (SUBLANES, 1), 0)
    levels = []
    m = chunk // 2
    while m >= 1:
        levels.append(m)
        m //= 2
    level_pairs = [(c2 < r2) & (r2 // (2 * m) == c2 // (2 * m)) & (r2 // m != c2 // m) for m in levels]
    same_row = r2 == c2

    def chunk_body(c, carry):
        sl = pl.ds(pl.multiple_of(c * chunk, chunk), chunk)
        z = f_ref[sl, :]
        logf = jnp.logaddexp(log_lb, log1m_lb + jax.nn.log_sigmoid(z))
        kk = (1.0 - lbh) * jax.nn.sigmoid(-z)
        q = q_ref[sl, :]
        v = i_ref[sl, :]
        b = _cumsum_rows(tril, logf)
        b_end = b[chunk - 1:chunk, :]
        qe = (q * jnp.exp(b)).astype(BF16)
        kd = (kk * jnp.exp(b_end - b)).astype(BF16)
        eb_end = jnp.exp(b_end)
        vb = v.astype(BF16)
        qb = q.astype(BF16)
        kb = kk.astype(BF16)

        lvl_q, lvl_k = [], []
        for m in levels:
            if m == 1:
                lvl_q.append((q * jnp.exp(logf)).astype(BF16))
                lvl_k.append(kb)
            else:
                fac = jnp.exp(-jnp.abs(b - _block_mid_rows(b, m, row_in_tile)))
                lvl_q.append((q * fac).astype(BF16))
                lvl_k.append((kk * fac).astype(BF16))

        for h in range(n_heads):
            hs = slice(h * HGRN_DIM, (h + 1) * HGRN_DIM)
            st = state_ref[h]
            o = _dot_nt(qe[:, hs], st.astype(BF16))
            scores = jnp.where(same_row, _dot_nt(qb[:, hs], kb[:, hs]), 0.0)
            for lq, lk, pairs in zip(lvl_q, lvl_k, level_pairs):
                scores = jnp.where(pairs, _dot_nt(lq[:, hs], lk[:, hs]), scores)
            o = o + _dot(scores.astype(BF16), vb[:, hs])

            state_ref[h] = st * eb_end[:, hs] + lax.dot_general(
                vb[:, hs], kd[:, hs], TN_DIMS, preferred_element_type=F32)
            y = _rms(o, ng[:, hs]) * _silu(g_ref[sl, hs])
            o_ref[sl, hs] = y.astype(o_ref.dtype)
        return carry

    lax.fori_loop(0, rows // chunk, chunk_body, 0, unroll=4)


def _hgrn(proj_a, lb_logits, norm_g, e, batch, seq):
    n = proj_a.shape[0]
    width = proj_a.shape[1] // 4
    n_layers = lb_logits.shape[0]
    rows = min(HGRN_ROWS, seq)
    steps = seq // rows
    chunk = min(HGRN_CHUNK, rows)
    col = lambda k: (lambda b, i: (b * steps + i, k))
    return pl.pallas_call(
        functools.partial(_hgrn_kernel, e=e, chunk=chunk),
        grid=(batch, steps),
        in_specs=[
            pl.BlockSpec((rows, width), col(0)),
            pl.BlockSpec((rows, width), col(1)),
            pl.BlockSpec((rows, width), col(2)),
            pl.BlockSpec((rows, width), col(3)),
            pl.BlockSpec((n_layers, width), lambda b, i: (0, 0)),
            pl.BlockSpec((None, 1, width), lambda b, i: (e, 0, 0)),
        ],
        out_specs=pl.BlockSpec((rows, width), col(0)),
        out_shape=jax.ShapeDtypeStruct((n, width), BF16),
        scratch_shapes=[pltpu.VMEM((width // HGRN_DIM, HGRN_DIM, HGRN_DIM), F32)],
        compiler_params=_params("parallel", "arbitrary"),
        name="hgrn2",
    )(proj_a, proj_a, proj_a, proj_a, lb_logits, norm_g)


def _causal_sweep(i, tile, n_groups, qst_ref, k_ref, v_ref, bias_fn, s_scr, mx_scr, ls_scr, acc_scr):
    lane_tiles = tile // LANES
    r2 = lax.broadcasted_iota(jnp.int32, (tile, tile), 0)
    c2 = lax.broadcasted_iota(jnp.int32, (tile, tile), 1)
    causal = c2 <= r2
    mx_scr[...] = jnp.full(mx_scr.shape, -jnp.inf, F32)
    ls_scr[...] = jnp.zeros(ls_scr.shape, F32)
    acc_scr[...] = jnp.zeros(acc_scr.shape, F32)

    def scores(j, masked):
        ks = pl.ds(pl.multiple_of(j * tile, tile), tile)
        for g in range(n_groups):
            gs = slice(g * LANES, (g + 1) * LANES)
            s = _dot_nt(qst_ref[g], k_ref[ks, gs])
            for half, bias in enumerate(bias_fn(g, j)):
                rs = slice(half * tile, (half + 1) * tile)
                sh = s[rs] * LOG2E + bias
                if masked:
                    sh = jnp.where(causal, sh, -jnp.inf)
                s_scr[g, j, rs, :] = sh
                part = sh[:, :LANES]
                for t in range(1, lane_tiles):
                    part = jnp.maximum(part, sh[:, t * LANES:(t + 1) * LANES])
                mx_scr[g, rs, :] = jnp.maximum(mx_scr[g, rs, :], part)

    def for_each_tile(count, fn):
        def pair(p, carry):
            fn(2 * p)
            fn(2 * p + 1)
            return carry

        lax.fori_loop(0, count // 2, pair, 0)

        @pl.when(count % 2 == 1)
        def _():
            fn(count - 1)

    for_each_tile(i, lambda j: scores(j, False))
    scores(i, True)

    for g in range(n_groups):
        row_max = jnp.max(mx_scr[g], axis=-1, keepdims=True)
        mx_scr[g] = jnp.broadcast_to(row_max, mx_scr.shape[1:])

    def accumulate(j):
        ks = pl.ds(pl.multiple_of(j * tile, tile), tile)
        for g in range(n_groups):
            gs = slice(g * LANES, (g + 1) * LANES)
            row_max = mx_scr[g]
            lsum = ls_scr[g]
            ps = []
            for t in range(lane_tiles):
                p = jnp.exp2(s_scr[g, j, :, t * LANES:(t + 1) * LANES] - row_max)
                lsum = lsum + p
                ps.append(p.astype(BF16))
            ls_scr[g] = lsum
            p = ps[0] if lane_tiles == 1 else jnp.concatenate(ps, axis=1)
            acc_scr[g] = acc_scr[g] + _dot(p, v_ref[ks, gs])

    for_each_tile(i + 1, accumulate)


def _sweep_scratch(n_groups, n_tiles, tile):
    return [
        pltpu.VMEM((n_groups, 2 * tile, LANES), BF16),
        pltpu.VMEM((n_groups, n_tiles, 2 * tile, tile), F32),
        pltpu.VMEM((n_groups, 2 * tile, LANES), F32),
        pltpu.VMEM((n_groups, 2 * tile, LANES), F32),
        pltpu.VMEM((n_groups, 2 * tile, LANES), F32),
    ]


def _stack_half_queries(q_ref, qst_ref, n_groups, half_dim, tile):
    first = lax.broadcasted_iota(jnp.int32, (1, LANES), 1) < half_dim
    for g in range(n_groups):
        q = q_ref[:, g * LANES:(g + 1) * LANES] * jnp.asarray(half_dim ** -0.5, BF16)
        zero = jnp.zeros_like(q)
        qst_ref[g, 0:tile, :] = jnp.where(first, q, zero)
        qst_ref[g, tile:2 * tile, :] = jnp.where(first, zero, q)
    return first


def _diff_kernel(q_ref, k_ref, v_ref, lam_ref, ng_ref, o_ref, qst_ref, s_scr, mx_scr, ls_scr, acc_scr,
                 *, lam_init, tile):
    i = pl.program_id(1)
    _stack_half_queries(q_ref, qst_ref, DIFF_HEADS, DIFF_QK_DIM, tile)
    r2 = lax.broadcasted_iota(jnp.int32, (tile, tile), 0)
    c2 = lax.broadcasted_iota(jnp.int32, (tile, tile), 1)
    key_minus_query = (c2 - r2).astype(F32)

    def alibi(g, j):
        slope = 2.0 ** (-8.0 * (g + 1) / DIFF_HEADS)
        bias = (slope * LOG2E) * (key_minus_query + ((j - i) * tile).astype(F32))
        return bias, bias

    _causal_sweep(i, tile, DIFF_HEADS, qst_ref, k_ref, v_ref, alibi, s_scr, mx_scr, ls_scr, acc_scr)

    lp = lam_ref[...]
    lam = (jnp.exp(jnp.sum(lp[0:1] * lp[1:2], axis=-1, keepdims=True))
           - jnp.exp(jnp.sum(lp[2:3] * lp[3:4], axis=-1, keepdims=True)) + lam_init)
    ng = ng_ref[...]
    for g in range(DIFF_HEADS):
        gs = slice(g * LANES, (g + 1) * LANES)
        o = acc_scr[g] / jnp.sum(ls_scr[g], axis=-1, keepdims=True)
        y = o[:tile] - lam * o[tile:]
        o_ref[:, gs] = (_rms(y, ng[:, gs]) * (1.0 - lam_init)).astype(o_ref.dtype)


def _diff_attn(proj_b, lam_p, norm_g, e, layer_idx, batch, seq):
    n = proj_b.shape[0]
    width = DIFF_HEADS * DIFF_V_DIM
    tile = min(ATTN_TILE, seq)
    nq = seq // tile
    lam_init = 0.8 - 0.6 * math.exp(-0.3 * layer_idx)
    return pl.pallas_call(
        functools.partial(_diff_kernel, lam_init=lam_init, tile=tile),
        grid=(batch, nq),
        in_specs=[
            pl.BlockSpec((tile, width), lambda b, i: (b * nq + i, 0)),
            pl.BlockSpec((seq, width), lambda b, i: (b, 1)),
            pl.BlockSpec((seq, width), lambda b, i: (b, 2)),
            pl.BlockSpec((None, 4, DIFF_QK_DIM), lambda b, i: (e, 0, 0)),
            pl.BlockSpec((None, 1, width), lambda b, i: (e, 0, 0)),
        ],
        out_specs=pl.BlockSpec((tile, width), lambda b, i: (b * nq + i, 0)),
        out_shape=jax.ShapeDtypeStruct((n, width), BF16),
        scratch_shapes=_sweep_scratch(DIFF_HEADS, nq, tile),
        compiler_params=_params("parallel", "arbitrary"),
        name="diff_attention",
    )(proj_b, proj_b, proj_b, lam_p, norm_g)


def _odd_proj_kernel(h_ref, g_ref, w_ref, fb_ref, qkv_ref, cf_ref, glu_ref, carry_ref, *, splits, steps_per_seq):
    @pl.when(pl.program_id(0) % steps_per_seq == 0)
    def _():
        carry_ref[...] = jnp.zeros_like(carry_ref)

    rows = h_ref.shape[0]
    xn = _rms(h_ref[...], g_ref[...]).astype(BF16)
    (q0, q1), (f0, f1), (c0, c1) = splits
    qkv_ref[...] = _dot(xn, w_ref[:, q0:q1]).astype(qkv_ref.dtype)
    glu_ref[...] = _dot(xn, w_ref[:, c0:c1])
    logf = jax.nn.log_sigmoid(_dot(xn, w_ref[:, f0:f1]) + fb_ref[...])
    tril = _tril_ones(LANES)
    total = carry_ref[...]
    for r0 in range(0, rows, LANES):
        cs = _cumsum_rows(tril, logf[r0:r0 + LANES]) + total
        cf_ref[r0:r0 + LANES, :] = cs
        total = cs[LANES - 1:LANES, :]
    carry_ref[...] = total


def _odd_proj(h, g, gi, w, wi, f_bias, splits, seq):
    n, d = h.shape
    cols = w.shape[-1]
    tm = min(ROW_TILE, seq)
    widths = [c1 - c0 for c0, c1 in splits]
    return pl.pallas_call(
        functools.partial(_odd_proj_kernel, splits=splits, steps_per_seq=seq // tm),
        grid=(n // tm,),
        in_specs=[
            pl.BlockSpec((tm, d), lambda i: (i, 0)),
            pl.BlockSpec((None, 1, d), lambda i: (gi, 0, 0)),
            _resident((None, d, cols), lambda i: (wi, 0, 0)),
            pl.BlockSpec((None, 1, widths[1]), lambda i: (wi, 0, 0)),
        ],
        out_specs=[pl.BlockSpec((tm, wd), lambda i: (i, 0)) for wd in widths],
        out_shape=[jax.ShapeDtypeStruct((n, wd), dt) for wd, dt in zip(widths, (BF16, F32, F32))],
        scratch_shapes=[pltpu.VMEM((1, widths[1]), F32)],
        compiler_params=_params("arbitrary"),
        name="odd_in_proj",
    )(h, g, w, f_bias)


def _fox_kernel(q_ref, k_ref, v_ref, cfc_ref, cfr_ref, o_ref, cfb_scr, qst_ref, s_scr, mx_scr, ls_scr, acc_scr,
                *, tile):
    i = pl.program_id(1)
    pairs = FOX_HEADS // 2
    first = _stack_half_queries(q_ref, qst_ref, pairs, FOX_DIM, tile)
    cfc = cfc_ref[...]
    for h in range(FOX_HEADS):
        cfb_scr[h] = jnp.broadcast_to(cfc[:, h:h + 1] * LOG2E, (tile, LANES))

    def forget_bias(g, j):
        cfr = cfr_ref[j] * LOG2E
        out = []
        for half in range(2):
            h = 2 * g + half
            cols = [cfb_scr[h] - cfr[h:h + 1, t * LANES:(t + 1) * LANES] for t in range(tile // LANES)]
            out.append(cols[0] if len(cols) == 1 else jnp.concatenate(cols, axis=1))
        return out

    _causal_sweep(i, tile, pairs, qst_ref, k_ref, v_ref, forget_bias, s_scr, mx_scr, ls_scr, acc_scr)

    for g in range(pairs):
        o = acc_scr[g] / jnp.sum(ls_scr[g], axis=-1, keepdims=True)
        o_ref[:, g * LANES:(g + 1) * LANES] = jnp.where(first, o[:tile], o[tile:]).astype(o_ref.dtype)


def _fox_attn(qkv, cf_col, cf_row, batch, seq):
    n = qkv.shape[0]
    width = FOX_HEADS * FOX_DIM
    tile = min(ATTN_TILE, seq)
    nq = seq // tile
    return pl.pallas_call(
        functools.partial(_fox_kernel, tile=tile),
        grid=(batch, nq),
        in_specs=[
            pl.BlockSpec((tile, width), lambda b, i: (b * nq + i, 0)),
            pl.BlockSpec((seq, width), lambda b, i: (b, 1)),
            pl.BlockSpec((seq, width), lambda b, i: (b, 2)),
            pl.BlockSpec((tile, LANES), lambda b, i: (b * nq + i, 0)),
            pl.BlockSpec((None, nq, FOX_HEADS, tile), lambda b, i: (b, 0, 0, 0)),
        ],
        out_specs=pl.BlockSpec((tile, width), lambda b, i: (b * nq + i, 0)),
        out_shape=jax.ShapeDtypeStruct((n, width), BF16),
        scratch_shapes=[pltpu.VMEM((FOX_HEADS, tile, LANES), F32)] + _sweep_scratch(FOX_HEADS // 2, nq, tile),
        compiler_params=_params("parallel", "arbitrary"),
        name="fox_attention",
    )(qkv, qkv, qkv, cf_col, cf_row)


def _conv_kernel(x_ref, w_ref, cb_ref, lg_ref, lb_ref, o_ref, u_ref, sh_ref, wb_ref):
    rows = x_ref.shape[0]
    ch = o_ref.shape[1]
    shifted_rows = sh_ref.shape[1]

    @pl.when(pl.program_id(1) == 0)
    def _():
        u_ref[0:CONV_HALO, :] = jnp.zeros((CONV_HALO, ch), F32)
        for j in range(CONV_WIDTH):
            wb_ref[j] = jnp.broadcast_to(w_ref[j:j + 1, :], (SUBLANES, ch))

    x = x_ref[...]
    u_ref[CONV_HALO:CONV_HALO + rows, :] = x[:, :ch] * jax.nn.sigmoid(x[:, ch:])
    for res in range(1, SUBLANES):
        sh_ref[res - 1] = u_ref[res:res + shifted_rows, :]
    first_tap = CONV_HALO - (CONV_WIDTH - 1)
    rt = CONV_ROW_TILE
    for r0 in range(0, rows, rt):
        acc = jnp.broadcast_to(cb_ref[...], (rt, ch))
        for j in range(CONV_WIDTH):
            res = (first_tap + j) % SUBLANES
            start = r0 + first_tap + j - res
            src = u_ref[start:start + rt, :] if res == 0 else sh_ref[res - 1, start:start + rt, :]
            acc = acc + jnp.concatenate([wb_ref[j]] * (rt // SUBLANES), axis=0) * src
        mu = jnp.mean(acc, axis=-1, keepdims=True)
        cen = acc - mu
        var = jnp.mean(cen * cen, axis=-1, keepdims=True)
        y = cen * lax.rsqrt(var + EPS) * lg_ref[...] + lb_ref[...]
        o_ref[r0:r0 + rt, :] = _silu(y).astype(o_ref.dtype)
    u_ref[0:CONV_HALO, :] = u_ref[rows:rows + CONV_HALO, :]


def _conv_module(glu_in, conv_w, conv_b, ln_g, ln_b, o, batch, seq):
    n = glu_in.shape[0]
    ch = glu_in.shape[1] // 2
    rows = min(CONV_ROWS, seq)
    steps = seq // rows
    vec = pl.BlockSpec((None, 1, ch), lambda b, i: (o, 0, 0))
    return pl.pallas_call(
        _conv_kernel,
        grid=(batch, steps),
        in_specs=[
            pl.BlockSpec((rows, 2 * ch), lambda b, i: (b * steps + i, 0)),
            pl.BlockSpec((None, CONV_WIDTH, ch), lambda b, i: (o, 0, 0)),
            vec, vec, vec,
        ],
        out_specs=pl.BlockSpec((rows, ch), lambda b, i: (b * steps + i, 0)),
        out_shape=jax.ShapeDtypeStruct((n, ch), BF16),
        scratch_shapes=[
            pltpu.VMEM((CONV_HALO + rows, ch), F32),
            pltpu.VMEM((SUBLANES - 1, CONV_HALO + rows - SUBLANES, ch), F32),
            pltpu.VMEM((CONV_WIDTH, SUBLANES, ch), F32),
        ],
        compiler_params=_params("parallel", "arbitrary"),
        name="conformer_conv",
    )(glu_in, conv_w, conv_b, ln_g, ln_b)


def _xattn_kernel(h_ref, ya_ref, yb_ref, wa_ref, wb_ref, gmix_ref, gpre_ref, gpost_ref, wq_ref, kv_ref, wo_ref,
                  o_ref):
    y_mix = _dot(ya_ref[...], wa_ref[...]) + _dot(yb_ref[...], wb_ref[...])
    h = h_ref[...] + _rms(y_mix, gmix_ref[...])
    d = h.shape[1]
    dh = d // XATTN_HEADS
    hn = _rms(h, gpre_ref[...]).astype(BF16)
    q = (_dot(hn, wq_ref[...]) * (dh ** -0.5)).astype(BF16)
    heads = []
    for hh in range(XATTN_HEADS):
        cs = slice(hh * dh, (hh + 1) * dh)
        s = _dot_nt(q[:, cs], kv_ref[:, cs])
        p = jnp.exp(s - jnp.max(s, axis=-1, keepdims=True))
        o = _dot(p.astype(BF16), kv_ref[:, d + hh * dh:d + (hh + 1) * dh])
        heads.append((o / jnp.sum(p, axis=-1, keepdims=True)).astype(BF16))
    y = _dot(jnp.concatenate(heads, axis=1), wo_ref[...])
    o_ref[...] = h + _rms(y, gpost_ref[...])


def _mix_out_xattn(h, ya, yb, w_out, e, gmix, kv, gpre, gpost, wq, wo, l, batch, seq):
    n, d = h.shape
    ka, kb = ya.shape[1], yb.shape[1]
    mem_len = kv.shape[0] // batch
    tm = min(ROW_TILE, seq)
    steps = seq // tm
    rows = lambda width: pl.BlockSpec((tm, width), lambda b, i: (b * steps + i, 0))
    vec = pl.BlockSpec((None, 1, d), lambda b, i: (l, 0, 0))
    return pl.pallas_call(
        _xattn_kernel,
        grid=(batch, steps),
        in_specs=[
            rows(d), rows(ka), rows(kb),
            _resident((None, None, ka, d), lambda b, i: (e, 0, 0, 0)),
            _resident((None, None, kb, d), lambda b, i: (e, 1, 0, 0)),
            vec, vec, vec,
            _resident((None, d, d), lambda b, i: (l, 0, 0)),
            pl.BlockSpec((mem_len, 2 * d), lambda b, i: (b, 0)),
            _resident((None, d, d), lambda b, i: (l, 0, 0)),
        ],
        out_specs=rows(d),
        out_shape=jax.ShapeDtypeStruct((n, d), F32),
        compiler_params=_params("parallel", "parallel"),
        name="mix_out_cross_attention",
    )(h, ya, yb, w_out, w_out, gmix, gpre, gpost, wq, kv, wo)


def kernel(x, mem, ffn_norm_pre, ffn_norm_post, ffn_w_gate, ffn_w_up, ffn_w_down, mix_norm_pre, mix_norm_post, even_w_in, even_w_out, hgrn_lb_logits, hgrn_norm_g, diff_lambda, diff_norm_g, odd_w_in, odd_w_out, fox_f_bias, conv_w, conv_b, conv_ln_g, conv_ln_b, xattn_norm_pre, xattn_norm_post, mem_norm_g, xattn_w_q, xattn_w_kv, xattn_w_o):
    batch, seq, d = x.shape
    depth = ffn_norm_pre.shape[0]
    mem_len = mem.shape[1]
    n = batch * seq
    row = lambda a: a[..., None, :]

    wg, wu, wd = ffn_w_gate.astype(BF16), ffn_w_up.astype(BF16), ffn_w_down.astype(BF16)
    gpre_ffn, gpost_ffn = row(ffn_norm_pre), row(ffn_norm_post)
    w_even_in = even_w_in.astype(BF16)
    hgrn_w = HGRN_HEADS * HGRN_DIM
    diff_w = DIFF_HEADS * DIFF_V_DIM
    fox_w = FOX_HEADS * FOX_DIM
    conv_ch = conv_w.shape[-1]
    mix_w = even_w_out.shape[1]
    w_even_out = even_w_out.astype(BF16).reshape(-1, 2, mix_w // 2, d)
    w_odd_out = odd_w_out.astype(BF16).reshape(-1, 2, mix_w // 2, d)
    f0 = 3 * fox_w
    w_odd_in = jnp.concatenate([
        odd_w_in[:, :, :f0],
        jnp.pad(odd_w_in[:, :, f0:f0 + FOX_HEADS], ((0, 0), (0, 0), (0, LANES - FOX_HEADS))),
        odd_w_in[:, :, f0 + FOX_HEADS:]], axis=-1).astype(BF16)
    fox_bias = row(jnp.pad(fox_f_bias, ((0, 0), (0, LANES - FOX_HEADS))))
    wq, wkv, wo = xattn_w_q.astype(BF16), xattn_w_kv.astype(BF16), xattn_w_o.astype(BF16)
    mem2 = mem.reshape(batch * mem_len, d)
    attn_tile = min(ATTN_TILE, seq)

    h = x.reshape(n, d)
    for l in range(depth):
        h = _ffn(h, gpre_ffn, gpost_ffn, wg, wu, wd, l, 0)
        if l % 2 == 0:
            e = l // 2
            proj_a, proj_b = _norm_proj(
                h, row(mix_norm_pre), l, w_even_in, e,
                splits=((0, 4 * hgrn_w), (4 * hgrn_w, 4 * hgrn_w + 3 * diff_w)),
                dtypes=(F32, BF16), name="even_in_proj")
            ya = _hgrn(proj_a, hgrn_lb_logits, row(hgrn_norm_g), e, batch, seq)
            yb = _diff_attn(proj_b, diff_lambda, row(diff_norm_g), e, l, batch, seq)
            w_mix_out, mix_idx = w_even_out, e
        else:
            o = l // 2
            qkv, cf_col, glu_in = _odd_proj(
                h, row(mix_norm_pre), l, w_odd_in, o, fox_bias,
                splits=((0, f0), (f0, f0 + LANES), (f0 + LANES, f0 + LANES + 2 * conv_ch)), seq=seq)
            cf_row = cf_col[:, :FOX_HEADS].reshape(batch, seq // attn_tile, attn_tile, FOX_HEADS)
            cf_row = cf_row.transpose(0, 1, 3, 2)
            ya = _fox_attn(qkv, cf_col, cf_row, batch, seq)
            yb = _conv_module(glu_in, conv_w, row(conv_b), row(conv_ln_g), row(conv_ln_b), o, batch, seq)
            w_mix_out, mix_idx = w_odd_out, o
        (kv,) = _norm_proj(mem2, row(mem_norm_g), l, wkv, l, splits=((0, 2 * d),), dtypes=(BF16,),
                           name="mem_kv_proj")
        h = _mix_out_xattn(h, ya, yb, w_mix_out, mix_idx, row(mix_norm_post), kv, row(xattn_norm_pre),
                           row(xattn_norm_post), wq, wo, l, batch, seq)
        h = _ffn(h, gpre_ffn, gpost_ffn, wg, wu, wd, l, 1)
    return h.reshape(batch, seq, d)
```

```python
import functools
import math

import jax
import jax.numpy as jnp
from jax import lax
from jax.experimental import pallas as pl
from jax.experimental.pallas import tpu as pltpu

F32 = jnp.float32
BF16 = jnp.bfloat16
EPS = 1e-6
LOG2E = math.log2(math.e)

HGRN_HEADS = 4
HGRN_DIM = 128
DIFF_HEADS = 4
DIFF_V_DIM = 128
DIFF_QK_DIM = 64
FOX_HEADS = 8
FOX_DIM = 64
CONV_WIDTH = 31
XATTN_HEADS = 4

LANES = 128
SUBLANES = 8
VMEM_LIMIT_BYTES = 56 * 1024 * 1024

ROW_TILE = 512
ATTN_TILE = 256
HGRN_CHUNK = 128
HGRN_ROWS = 512
CONV_ROWS = 256
CONV_HALO = 32
CONV_ROW_TILE = 32

NT_DIMS = (((1,), (1,)), ((), ()))
TN_DIMS = (((0,), (0,)), ((), ()))


def _params(*semantics):
    return pltpu.CompilerParams(dimension_semantics=semantics, vmem_limit_bytes=VMEM_LIMIT_BYTES)


def _resident(block_shape, index_map):
    return pl.BlockSpec(block_shape, index_map, pipeline_mode=pl.Buffered(1))


def _rms(x, g):
    return x * lax.rsqrt(jnp.mean(x * x, axis=-1, keepdims=True) + EPS) * g


def _dot(a, b):
    return jnp.dot(a, b, preferred_element_type=F32)


def _dot_nt(a, b):
    return lax.dot_general(a, b, NT_DIMS, preferred_element_type=F32)


def _silu(x):
    return x * jax.nn.sigmoid(x)


def _split3(x):
    hi = x.astype(BF16)
    r1 = x - hi.astype(F32)
    mid = r1.astype(BF16)
    lo = (r1 - mid.astype(F32)).astype(BF16)
    return hi, mid, lo


def _cumsum_rows(tril, x):
    hi, mid, lo = _split3(x)
    return _dot(tril, hi) + _dot(tril, mid) + _dot(tril, lo)


def _tril_ones(n):
    r = lax.broadcasted_iota(jnp.int32, (n, n), 0)
    c = lax.broadcasted_iota(jnp.int32, (n, n), 1)
    return jnp.where(r >= c, 1.0, 0.0).astype(BF16)


def _ffn_kernel(h_ref, gpre_ref, gpost_ref, wg_ref, wu_ref, wd_ref, o_ref):
    h = h_ref[...]
    xn = _rms(h, gpre_ref[...]).astype(BF16)
    g = _dot(xn, wg_ref[...])
    u = _dot(xn, wu_ref[...])
    a = (_silu(g) * u).astype(BF16)
    y = _dot(a, wd_ref[...])
    o_ref[...] = h + 0.5 * _rms(y, gpost_ref[...])


def _ffn(h, gpre, gpost, wg, wu, wd, l, j):
    n, d = h.shape
    f = wg.shape[-1]
    tm = min(ROW_TILE, n)
    sel = lambda i: (l, j, 0, 0)
    return pl.pallas_call(
        _ffn_kernel,
        grid=(n // tm,),
        in_specs=[
            pl.BlockSpec((tm, d), lambda i: (i, 0)),
            pl.BlockSpec((None, None, 1, d), sel),
            pl.BlockSpec((None, None, 1, d), sel),
            _resident((None, None, d, f), sel),
            _resident((None, None, d, f), sel),
            _resident((None, None, f, d), sel),
        ],
        out_specs=pl.BlockSpec((tm, d), lambda i: (i, 0)),
        out_shape=jax.ShapeDtypeStruct((n, d), F32),
        compiler_params=_params("parallel"),
        name="ffn_half",
    )(h, gpre, gpost, wg, wu, wd)


def _norm_proj_kernel(h_ref, g_ref, w_ref, *o_refs, splits):
    xn = _rms(h_ref[...], g_ref[...]).astype(BF16)
    for o_ref, (c0, c1) in zip(o_refs, splits):
        o_ref[...] = _dot(xn, w_ref[:, c0:c1]).astype(o_ref.dtype)


def _norm_proj(h, g, gi, w, wi, splits, dtypes, name):
    n, d = h.shape
    cols = w.shape[-1]
    tm = min(ROW_TILE, n)
    return pl.pallas_call(
        functools.partial(_norm_proj_kernel, splits=splits),
        grid=(n // tm,),
        in_specs=[
            pl.BlockSpec((tm, d), lambda i: (i, 0)),
            pl.BlockSpec((None, 1, d), lambda i: (gi, 0, 0)),
            _resident((None, d, cols), lambda i: (wi, 0, 0)),
        ],
        out_specs=[pl.BlockSpec((tm, c1 - c0), lambda i: (i, 0)) for c0, c1 in splits],
        out_shape=[jax.ShapeDtypeStruct((n, c1 - c0), dt) for (c0, c1), dt in zip(splits, dtypes)],
        compiler_params=_params("parallel"),
        name=name,
    )(h, g, w)


def _block_mid_rows(b, m, row_in_tile):
    chunk, width = b.shape
    if m >= SUBLANES:
        parts = [jnp.broadcast_to(b[r:r + 1, :], (2 * m, width)) for r in range(m - 1, chunk, 2 * m)]
    else:
        parts = []
        for base in range(0, chunk, SUBLANES):
            tile = None
            for off in range(0, SUBLANES, 2 * m):
                r = base + off + m - 1
                cand = jnp.broadcast_to(b[r:r + 1, :], (SUBLANES, width))
                tile = cand if tile is None else jnp.where(row_in_tile >= off, cand, tile)
            parts.append(tile)
    return parts[0] if len(parts) == 1 else jnp.concatenate(parts, axis=0)


def _hgrn_kernel(q_ref, f_ref, i_ref, g_ref, lbl_ref, ng_ref, o_ref, state_ref, *, e, chunk):
    rows = q_ref.shape[0]
    width = q_ref.shape[1]
    n_heads = width // HGRN_DIM

    @pl.when(pl.program_id(1) == 0)
    def _():
        state_ref[...] = jnp.zeros_like(state_ref)

    lbl = lbl_ref[...]
    ex = jnp.exp(lbl - jnp.max(lbl, axis=0, keepdims=True))
    sm = ex / jnp.sum(ex, axis=0, keepdims=True)
    lb = jnp.sum(sm[: e + 1], axis=0, keepdims=True) - sm[0:1]
    lbh = jnp.maximum(lb, 0.0)
    log_lb = jnp.log(lbh)
    log1m_lb = jnp.log1p(-lbh)
    ng = ng_ref[...]

    tril = _tril_ones(chunk)
    r2 = lax.broadcasted_iota(jnp.int32, (chunk, chunk), 0)
    c2 = lax.broadcasted_iota(jnp.int32, (chunk, chunk), 1)
    row_in_tile = lax.broadcasted_iota(jnp.int32, (SUBLANES, 1), 0)
    levels = []
    m = chunk // 2
    while m >= 1:
        levels.append(m)
        m //= 2
    level_pairs = [(c2 < r2) & (r2 // (2 * m) == c2 // (2 * m)) & (r2 // m != c2 // m) for m in levels]
    same_row = r2 == c2

    def chunk_body(c, carry):
        sl = pl.ds(pl.multiple_of(c * chunk, chunk), chunk)
        z = f_ref[sl, :]
        logf = jnp.logaddexp(log_lb, log1m_lb + jax.nn.log_sigmoid(z))
        kk = (1.0 - lbh) * jax.nn.sigmoid(-z)
        q = q_ref[sl, :]
        v = i_ref[sl, :]
        b = _cumsum_rows(tril, logf)
        b_end = b[chunk - 1:chunk, :]
        qe = (q * jnp.exp(b)).astype(BF16)
        kd = (kk * jnp.exp(b_end - b)).astype(BF16)
        eb_end = jnp.exp(b_end)
        vb = v.astype(BF16)
        qb = q.astype(BF16)
        kb = kk.astype(BF16)

        lvl_q, lvl_k = [], []
        for m in levels:
            if m == 1:
                lvl_q.append((q * jnp.exp(logf)).astype(BF16))
                lvl_k.append(kb)
            else:
                fac = jnp.exp(-jnp.abs(b - _block_mid_rows(b, m, row_in_tile)))
                lvl_q.append((q * fac).astype(BF16))
                lvl_k.append((kk * fac).astype(BF16))

        for h in range(n_heads):
            hs = slice(h * HGRN_DIM, (h + 1) * HGRN_DIM)
            st = state_ref[h]
            o = _dot_nt(qe[:, hs], st.astype(BF16))
            scores = jnp.where(same_row, _dot_nt(qb[:, hs], kb[:, hs]), 0.0)
            for lq, lk, pairs in zip(lvl_q, lvl_k, level_pairs):
                scores = jnp.where(pairs, _dot_nt(lq[:, hs], lk[:, hs]), scores)
            o = o + _dot(scores.astype(BF16), vb[:, hs])

            state_ref[h] = st * eb_end[:, hs] + lax.dot_general(
                vb[:, hs], kd[:, hs], TN_DIMS, preferred_element_type=F32)
            y = _rms(o, ng[:, hs]) * _silu(g_ref[sl, hs])
            o_ref[sl, hs] = y.astype(o_ref.dtype)
        return carry

    lax.fori_loop(0, rows // chunk, chunk_body, 0, unroll=4)


def _hgrn(proj_a, lb_logits, norm_g, e, batch, seq):
    n = proj_a.shape[0]
    width = proj_a.shape[1] // 4
    n_layers = lb_logits.shape[0]
    rows = min(HGRN_ROWS, seq)
    steps = seq // rows
    chunk = min(HGRN_CHUNK, rows)
    col = lambda k: (lambda b, i: (b * steps + i, k))
    return pl.pallas_call(
        functools.partial(_hgrn_kernel, e=e, chunk=chunk),
        grid=(batch, steps),
        in_specs=[
            pl.BlockSpec((rows, width), col(0)),
            pl.BlockSpec((rows, width), col(1)),
            pl.BlockSpec((rows, width), col(2)),
            pl.BlockSpec((rows, width), col(3)),
            pl.BlockSpec((n_layers, width), lambda b, i: (0, 0)),
            pl.BlockSpec((None, 1, width), lambda b, i: (e, 0, 0)),
        ],
        out_specs=pl.BlockSpec((rows, width), col(0)),
        out_shape=jax.ShapeDtypeStruct((n, width), BF16),
        scratch_shapes=[pltpu.VMEM((width // HGRN_DIM, HGRN_DIM, HGRN_DIM), F32)],
        compiler_params=_params("parallel", "arbitrary"),
        name="hgrn2",
    )(proj_a, proj_a, proj_a, proj_a, lb_logits, norm_g)


def _causal_sweep(i, tile, n_groups, qst_ref, k_ref, v_ref, bias_fn, s_scr, mx_scr, ls_scr, acc_scr):
    lane_tiles = tile // LANES
    r2 = lax.broadcasted_iota(jnp.int32, (tile, tile), 0)
    c2 = lax.broadcasted_iota(jnp.int32, (tile, tile), 1)
    causal = c2 <= r2
    mx_scr[...] = jnp.full(mx_scr.shape, -jnp.inf, F32)
    ls_scr[...] = jnp.zeros(ls_scr.shape, F32)
    acc_scr[...] = jnp.zeros(acc_scr.shape, F32)

    def scores(j, masked):
        ks = pl.ds(pl.multiple_of(j * tile, tile), tile)
        for g in range(n_groups):
            gs = slice(g * LANES, (g + 1) * LANES)
            s = _dot_nt(qst_ref[g], k_ref[ks, gs])
            for half, bias in enumerate(bias_fn(g, j)):
                rs = slice(half * tile, (half + 1) * tile)
                sh = s[rs] * LOG2E + bias
                if masked:
                    sh = jnp.where(causal, sh, -jnp.inf)
                s_scr[g, j, rs, :] = sh
                part = sh[:, :LANES]
                for t in range(1, lane_tiles):
                    part = jnp.maximum(part, sh[:, t * LANES:(t + 1) * LANES])
                mx_scr[g, rs, :] = jnp.maximum(mx_scr[g, rs, :], part)

    def for_each_tile(count, fn):
        def pair(p, carry):
            fn(2 * p)
            fn(2 * p + 1)
            return carry

        lax.fori_loop(0, count // 2, pair, 0)

        @pl.when(count % 2 == 1)
        def _():
            fn(count - 1)

    for_each_tile(i, lambda j: scores(j, False))
    scores(i, True)

    for g in range(n_groups):
        row_max = jnp.max(mx_scr[g], axis=-1, keepdims=True)
        mx_scr[g] = jnp.broadcast_to(row_max, mx_scr.shape[1:])

    def accumulate(j):
        ks = pl.ds(pl.multiple_of(j * tile, tile), tile)
        for g in range(n_groups):
            gs = slice(g * LANES, (g + 1) * LANES)
            row_max = mx_scr[g]
            lsum = ls_scr[g]
            ps = []
            for t in range(lane_tiles):
                p = jnp.exp2(s_scr[g, j, :, t * LANES:(t + 1) * LANES] - row_max)
                lsum = lsum + p
                ps.append(p.astype(BF16))
            ls_scr[g] = lsum
            p = ps[0] if lane_tiles == 1 else jnp.concatenate(ps, axis=1)
            acc_scr[g] = acc_scr[g] + _dot(p, v_ref[ks, gs])

    for_each_tile(i + 1, accumulate)


def _sweep_scratch(n_groups, n_tiles, tile):
    return [
        pltpu.VMEM((n_groups, 2 * tile, LANES), BF16),
        pltpu.VMEM((n_groups, n_tiles, 2 * tile, tile), F32),
        pltpu.VMEM((n_groups, 2 * tile, LANES), F32),
        pltpu.VMEM((n_groups, 2 * tile, LANES), F32),
        pltpu.VMEM((n_groups, 2 * tile, LANES), F32),
    ]


def _stack_half_queries(q_ref, qst_ref, n_groups, half_dim, tile):
    first = lax.broadcasted_iota(jnp.int32, (1, LANES), 1) < half_dim
    for g in range(n_groups):
        q = q_ref[:, g * LANES:(g + 1) * LANES] * jnp.asarray(half_dim ** -0.5, BF16)
        zero = jnp.zeros_like(q)
        qst_ref[g, 0:tile, :] = jnp.where(first, q, zero)
        qst_ref[g, tile:2 * tile, :] = jnp.where(first, zero, q)
    return first


def _diff_kernel(q_ref, k_ref, v_ref, lam_ref, ng_ref, o_ref, qst_ref, s_scr, mx_scr, ls_scr, acc_scr,
                 *, lam_init, tile):
    i = pl.program_id(1)
    _stack_half_queries(q_ref, qst_ref, DIFF_HEADS, DIFF_QK_DIM, tile)
    r2 = lax.broadcasted_iota(jnp.int32, (tile, tile), 0)
    c2 = lax.broadcasted_iota(jnp.int32, (tile, tile), 1)
    key_minus_query = (c2 - r2).astype(F32)

    def alibi(g, j):
        slope = 2.0 ** (-8.0 * (g + 1) / DIFF_HEADS)
        bias = (slope * LOG2E) * (key_minus_query + ((j - i) * tile).astype(F32))
        return bias, bias

    _causal_sweep(i, tile, DIFF_HEADS, qst_ref, k_ref, v_ref, alibi, s_scr, mx_scr, ls_scr, acc_scr)

    lp = lam_ref[...]
    lam = (jnp.exp(jnp.sum(lp[0:1] * lp[1:2], axis=-1, keepdims=True))
           - jnp.exp(jnp.sum(lp[2:3] * lp[3:4], axis=-1, keepdims=True)) + lam_init)
    ng = ng_ref[...]
    for g in range(DIFF_HEADS):
        gs = slice(g * LANES, (g + 1) * LANES)
        o = acc_scr[g] / jnp.sum(ls_scr[g], axis=-1, keepdims=True)
        y = o[:tile] - lam * o[tile:]
        o_ref[:, gs] = (_rms(y, ng[:, gs]) * (1.0 - lam_init)).astype(o_ref.dtype)


def _diff_attn(proj_b, lam_p, norm_g, e, layer_idx, batch, seq):
    n = proj_b.shape[0]
    width = DIFF_HEADS * DIFF_V_DIM
    tile = min(ATTN_TILE, seq)
    nq = seq // tile
    lam_init = 0.8 - 0.6 * math.exp(-0.3 * layer_idx)
    return pl.pallas_call(
        functools.partial(_diff_kernel, lam_init=lam_init, tile=tile),
        grid=(batch, nq),
        in_specs=[
            pl.BlockSpec((tile, width), lambda b, i: (b * nq + i, 0)),
            pl.BlockSpec((seq, width), lambda b, i: (b, 1)),
            pl.BlockSpec((seq, width), lambda b, i: (b, 2)),
            pl.BlockSpec((None, 4, DIFF_QK_DIM), lambda b, i: (e, 0, 0)),
            pl.BlockSpec((None, 1, width), lambda b, i: (e, 0, 0)),
        ],
        out_specs=pl.BlockSpec((tile, width), lambda b, i: (b * nq + i, 0)),
        out_shape=jax.ShapeDtypeStruct((n, width), BF16),
        scratch_shapes=_sweep_scratch(DIFF_HEADS, nq, tile),
        compiler_params=_params("parallel", "arbitrary"),
        name="diff_attention",
    )(proj_b, proj_b, proj_b, lam_p, norm_g)


def _odd_proj_kernel(h_ref, g_ref, w_ref, fb_ref, qkv_ref, cf_ref, glu_ref, carry_ref, *, splits, steps_per_seq):
    @pl.when(pl.program_id(0) % steps_per_seq == 0)
    def _():
        carry_ref[...] = jnp.zeros_like(carry_ref)

    rows = h_ref.shape[0]
    xn = _rms(h_ref[...], g_ref[...]).astype(BF16)
    (q0, q1), (f0, f1), (c0, c1) = splits
    qkv_ref[...] = _dot(xn, w_ref[:, q0:q1]).astype(qkv_ref.dtype)
    glu_ref[...] = _dot(xn, w_ref[:, c0:c1])
    logf = jax.nn.log_sigmoid(_dot(xn, w_ref[:, f0:f1]) + fb_ref[...])
    tril = _tril_ones(LANES)
    total = carry_ref[...]
    for r0 in range(0, rows, LANES):
        cs = _cumsum_rows(tril, logf[r0:r0 + LANES]) + total
        cf_ref[r0:r0 + LANES, :] = cs
        total = cs[LANES - 1:LANES, :]
    carry_ref[...] = total


def _odd_proj(h, g, gi, w, wi, f_bias, splits, seq):
    n, d = h.shape
    cols = w.shape[-1]
    tm = min(ROW_TILE, seq)
    widths = [c1 - c0 for c0, c1 in splits]
    return pl.pallas_call(
        functools.partial(_odd_proj_kernel, splits=splits, steps_per_seq=seq // tm),
        grid=(n // tm,),
        in_specs=[
            pl.BlockSpec((tm, d), lambda i: (i, 0)),
            pl.BlockSpec((None, 1, d), lambda i: (gi, 0, 0)),
            _resident((None, d, cols), lambda i: (wi, 0, 0)),
            pl.BlockSpec((None, 1, widths[1]), lambda i: (wi, 0, 0)),
        ],
        out_specs=[pl.BlockSpec((tm, wd), lambda i: (i, 0)) for wd in widths],
        out_shape=[jax.ShapeDtypeStruct((n, wd), dt) for wd, dt in zip(widths, (BF16, F32, F32))],
        scratch_shapes=[pltpu.VMEM((1, widths[1]), F32)],
        compiler_params=_params("arbitrary"),
        name="odd_in_proj",
    )(h, g, w, f_bias)


def _fox_kernel(q_ref, k_ref, v_ref, cfc_ref, cfr_ref, o_ref, cfb_scr, qst_ref, s_scr, mx_scr, ls_scr, acc_scr,
                *, tile):
    i = pl.program_id(1)
    pairs = FOX_HEADS // 2
    first = _stack_half_queries(q_ref, qst_ref, pairs, FOX_DIM, tile)
    cfc = cfc_ref[...]
    for h in range(FOX_HEADS):
        cfb_scr[h] = jnp.broadcast_to(cfc[:, h:h + 1] * LOG2E, (tile, LANES))

    def forget_bias(g, j):
        cfr = cfr_ref[j] * LOG2E
        out = []
        for half in range(2):
            h = 2 * g + half
            cols = [cfb_scr[h] - cfr[h:h + 1, t * LANES:(t + 1) * LANES] for t in range(tile // LANES)]
            out.append(cols[0] if len(cols) == 1 else jnp.concatenate(cols, axis=1))
        return out

    _causal_sweep(i, tile, pairs, qst_ref, k_ref, v_ref, forget_bias, s_scr, mx_scr, ls_scr, acc_scr)

    for g in range(pairs):
        o = acc_scr[g] / jnp.sum(ls_scr[g], axis=-1, keepdims=True)
        o_ref[:, g * LANES:(g + 1) * LANES] = jnp.where(first, o[:tile], o[tile:]).astype(o_ref.dtype)


def _fox_attn(qkv, cf_col, cf_row, batch, seq):
    n = qkv.shape[0]
    width = FOX_HEADS * FOX_DIM
    tile = min(ATTN_TILE, seq)
    nq = seq // tile
    return pl.pallas_call(
        functools.partial(_fox_kernel, tile=tile),
        grid=(batch, nq),
        in_specs=[
            pl.BlockSpec((tile, width), lambda b, i: (b * nq + i, 0)),
            pl.BlockSpec((seq, width), lambda b, i: (b, 1)),
            pl.BlockSpec((seq, width), lambda b, i: (b, 2)),
            pl.BlockSpec((tile, LANES), lambda b, i: (b * nq + i, 0)),
            pl.BlockSpec((None, nq, FOX_HEADS, tile), lambda b, i: (b, 0, 0, 0)),
        ],
        out_specs=pl.BlockSpec((tile, width), lambda b, i: (b * nq + i, 0)),
        out_shape=jax.ShapeDtypeStruct((n, width), BF16),
        scratch_shapes=[pltpu.VMEM((FOX_HEADS, tile, LANES), F32)] + _sweep_scratch(FOX_HEADS // 2, nq, tile),
        compiler_params=_params("parallel", "arbitrary"),
        name="fox_attention",
    )(qkv, qkv, qkv, cf_col, cf_row)


def _conv_kernel(x_ref, w_ref, cb_ref, lg_ref, lb_ref, o_ref, u_ref, sh_ref, wb_ref):
    rows = x_ref.shape[0]
    ch = o_ref.shape[1]
    shifted_rows = sh_ref.shape[1]

    @pl.when(pl.program_id(1) == 0)
    def _():
        u_ref[0:CONV_HALO, :] = jnp.zeros((CONV_HALO, ch), F32)
        for j in range(CONV_WIDTH):
            wb_ref[j] = jnp.broadcast_to(w_ref[j:j + 1, :], (SUBLANES, ch))

    x = x_ref[...]
    u_ref[CONV_HALO:CONV_HALO + rows, :] = x[:, :ch] * jax.nn.sigmoid(x[:, ch:])
    for res in range(1, SUBLANES):
        sh_ref[res - 1] = u_ref[res:res + shifted_rows, :]
    first_tap = CONV_HALO - (CONV_WIDTH - 1)
    rt = CONV_ROW_TILE
    for r0 in range(0, rows, rt):
        acc = jnp.broadcast_to(cb_ref[...], (rt, ch))
        for j in range(CONV_WIDTH):
            res = (first_tap + j) % SUBLANES
            start = r0 + first_tap + j - res
            src = u_ref[start:start + rt, :] if res == 0 else sh_ref[res - 1, start:start + rt, :]
            acc = acc + jnp.concatenate([wb_ref[j]] * (rt // SUBLANES), axis=0) * src
        mu = jnp.mean(acc, axis=-1, keepdims=True)
        cen = acc - mu
        var = jnp.mean(cen * cen, axis=-1, keepdims=True)
        y = cen * lax.rsqrt(var + EPS) * lg_ref[...] + lb_ref[...]
        o_ref[r0:r0 + rt, :] = _silu(y).astype(o_ref.dtype)
    u_ref[0:CONV_HALO, :] = u_ref[rows:rows + CONV_HALO, :]


def _conv_module(glu_in, conv_w, conv_b, ln_g, ln_b, o, batch, seq):
    n = glu_in.shape[0]
    ch = glu_in.shape[1] // 2
    rows = min(CONV_ROWS, seq)
    steps = seq // rows
    vec = pl.BlockSpec((None, 1, ch), lambda b, i: (o, 0, 0))
    return pl.pallas_call(
        _conv_kernel,
        grid=(batch, steps),
        in_specs=[
            pl.BlockSpec((rows, 2 * ch), lambda b, i: (b * steps + i, 0)),
            pl.BlockSpec((None, CONV_WIDTH, ch), lambda b, i: (o, 0, 0)),
            vec, vec, vec,
        ],
        out_specs=pl.BlockSpec((rows, ch), lambda b, i: (b * steps + i, 0)),
        out_shape=jax.ShapeDtypeStruct((n, ch), BF16),
        scratch_shapes=[
            pltpu.VMEM((CONV_HALO + rows, ch), F32),
            pltpu.VMEM((SUBLANES - 1, CONV_HALO + rows - SUBLANES, ch), F32),
            pltpu.VMEM((CONV_WIDTH, SUBLANES, ch), F32),
        ],
        compiler_params=_params("parallel", "arbitrary"),
        name="conformer_conv",
    )(glu_in, conv_w, conv_b, ln_g, ln_b)


def _xattn_kernel(h_ref, ya_ref, yb_ref, wa_ref, wb_ref, gmix_ref, gpre_ref, gpost_ref, wq_ref, kv_ref, wo_ref,
                  o_ref):
    y_mix = _dot(ya_ref[...], wa_ref[...]) + _dot(yb_ref[...], wb_ref[...])
    h = h_ref[...] + _rms(y_mix, gmix_ref[...])
    d = h.shape[1]
    dh = d // XATTN_HEADS
    hn = _rms(h, gpre_ref[...]).astype(BF16)
    q = (_dot(hn, wq_ref[...]) * (dh ** -0.5)).astype(BF16)
    heads = []
    for hh in range(XATTN_HEADS):
        cs = slice(hh * dh, (hh + 1) * dh)
        s = _dot_nt(q[:, cs], kv_ref[:, cs])
        p = jnp.exp(s - jnp.max(s, axis=-1, keepdims=True))
        o = _dot(p.astype(BF16), kv_ref[:, d + hh * dh:d + (hh + 1) * dh])
        heads.append((o / jnp.sum(p, axis=-1, keepdims=True)).astype(BF16))
    y = _dot(jnp.concatenate(heads, axis=1), wo_ref[...])
    o_ref[...] = h + _rms(y, gpost_ref[...])


def _mix_out_xattn(h, ya, yb, w_out, e, gmix, kv, gpre, gpost, wq, wo, l, batch, seq):
    n, d = h.shape
    ka, kb = ya.shape[1], yb.shape[1]
    mem_len = kv.shape[0] // batch
    tm = min(ROW_TILE, seq)
    steps = seq // tm
    rows = lambda width: pl.BlockSpec((tm, width), lambda b, i: (b * steps + i, 0))
    vec = pl.BlockSpec((None, 1, d), lambda b, i: (l, 0, 0))
    return pl.pallas_call(
        _xattn_kernel,
        grid=(batch, steps),
        in_specs=[
            rows(d), rows(ka), rows(kb),
            _resident((None, None, ka, d), lambda b, i: (e, 0, 0, 0)),
            _resident((None, None, kb, d), lambda b, i: (e, 1, 0, 0)),
            vec, vec, vec,
            _resident((None, d, d), lambda b, i: (l, 0, 0)),
            pl.BlockSpec((mem_len, 2 * d), lambda b, i: (b, 0)),
            _resident((None, d, d), lambda b, i: (l, 0, 0)),
        ],
        out_specs=rows(d),
        out_shape=jax.ShapeDtypeStruct((n, d), F32),
        compiler_params=_params("parallel", "parallel"),
        name="mix_out_cross_attention",
    )(h, ya, yb, w_out, w_out, gmix, gpre, gpost, wq, kv, wo)


def kernel(x, mem, ffn_norm_pre, ffn_norm_post, ffn_w_gate, ffn_w_up, ffn_w_down, mix_norm_pre, mix_norm_post, even_w_in, even_w_out, hgrn_lb_logits, hgrn_norm_g, diff_lambda, diff_norm_g, odd_w_in, odd_w_out, fox_f_bias, conv_w, conv_b, conv_ln_g, conv_ln_b, xattn_norm_pre, xattn_norm_post, mem_norm_g, xattn_w_q, xattn_w_kv, xattn_w_o):
    batch, seq, d = x.shape
    depth = ffn_norm_pre.shape[0]
    mem_len = mem.shape[1]
    n = batch * seq
    row = lambda a: a[..., None, :]

    wg, wu, wd = ffn_w_gate.astype(BF16), ffn_w_up.astype(BF16), ffn_w_down.astype(BF16)
    gpre_ffn, gpost_ffn = row(ffn_norm_pre), row(ffn_norm_post)
    w_even_in = even_w_in.astype(BF16)
    hgrn_w = HGRN_HEADS * HGRN_DIM
    diff_w = DIFF_HEADS * DIFF_V_DIM
    fox_w = FOX_HEADS * FOX_DIM
    conv_ch = conv_w.shape[-1]
    mix_w = even_w_out.shape[1]
    w_even_out = even_w_out.astype(BF16).reshape(-1, 2, mix_w // 2, d)
    w_odd_out = odd_w_out.astype(BF16).reshape(-1, 2, mix_w // 2, d)
    f0 = 3 * fox_w
    w_odd_in = jnp.concatenate([
        odd_w_in[:, :, :f0],
        jnp.pad(odd_w_in[:, :, f0:f0 + FOX_HEADS], ((0, 0), (0, 0), (0, LANES - FOX_HEADS))),
        odd_w_in[:, :, f0 + FOX_HEADS:]], axis=-1).astype(BF16)
    fox_bias = row(jnp.pad(fox_f_bias, ((0, 0), (0, LANES - FOX_HEADS))))
    wq, wkv, wo = xattn_w_q.astype(BF16), xattn_w_kv.astype(BF16), xattn_w_o.astype(BF16)
    mem2 = mem.reshape(batch * mem_len, d)
    attn_tile = min(ATTN_TILE, seq)

    h = x.reshape(n, d)
    for l in range(depth):
        h = _ffn(h, gpre_ffn, gpost_ffn, wg, wu, wd, l, 0)
        if l % 2 == 0:
            e = l // 2
            proj_a, proj_b = _norm_proj(
                h, row(mix_norm_pre), l, w_even_in, e,
                splits=((0, 4 * hgrn_w), (4 * hgrn_w, 4 * hgrn_w + 3 * diff_w)),
                dtypes=(F32, BF16), name="even_in_proj")
            ya = _hgrn(proj_a, hgrn_lb_logits, row(hgrn_norm_g), e, batch, seq)
            yb = _diff_attn(proj_b, diff_lambda, row(diff_norm_g), e, l, batch, seq)
            w_mix_out, mix_idx = w_even_out, e
        else:
            o = l // 2
            qkv, cf_col, glu_in = _odd_proj(
                h, row(mix_norm_pre), l, w_odd_in, o, fox_bias,
                splits=((0, f0), (f0, f0 + LANES), (f0 + LANES, f0 + LANES + 2 * conv_ch)), seq=seq)
            cf_row = cf_col[:, :FOX_HEADS].reshape(batch, seq // attn_tile, attn_tile, FOX_HEADS)
            cf_row = cf_row.transpose(0, 1, 3, 2)
            ya = _fox_attn(qkv, cf_col, cf_row, batch, seq)
            yb = _conv_module(glu_in, conv_w, row(conv_b), row(conv_ln_g), row(conv_ln_b), o, batch, seq)
            w_mix_out, mix_idx = w_odd_out, o
        (kv,) = _norm_proj(mem2, row(mem_norm_g), l, wkv, l, splits=((0, 2 * d),), dtypes=(BF16,),
                           name="mem_kv_proj")
        h = _mix_out_xattn(h, ya, yb, w_mix_out, mix_idx, row(mix_norm_post), kv, row(xattn_norm_pre),
                           row(xattn_norm_post), wq, wo, l, batch, seq)
        h = _ffn(h, gpre_ffn, gpost_ffn, wg, wu, wd, l, 1)
    return h.reshape(batch, seq, d)
```
